```python
import math
import jax
import jax.numpy as jnp
from jax import lax
import numpy as np

D_MODEL = 1024
BATCH = 16
SEQ = 2048
DEPTH = 2

HEAD_DIM = 64
DIL_CONFIGS = ((128, 1), (512, 4), (2048, 16))
N_DIL_GROUPS = 3
DIL_HEADS = 4
DIL_WIDTH = DIL_HEADS * HEAD_DIM
MOBA_HEADS = 8
MOBA_WIDTH = MOBA_HEADS * HEAD_DIM
MOBA_BLOCK = 256
MOBA_TOPK = 3
MOBA_QCHUNK = 16
RMS_EPS = 1e-6
NEG_INF = -1e30
SPLIT_SIZES = (N_DIL_GROUPS * 3 * DIL_WIDTH, DIL_WIDTH, 3 * MOBA_WIDTH, MOBA_WIDTH, D_MODEL, D_MODEL)
IN_COLS = N_DIL_GROUPS * 3 * DIL_WIDTH + DIL_WIDTH + 4 * MOBA_WIDTH + 2 * D_MODEL

kernel_name = 'hybrid_dilated_moba_gated'


def alibi_slopes(n):
    return jnp.asarray(2.0 ** (-8.0 * np.arange(1, n + 1) / n), dtype=jnp.float32)


def rmsnorm(x, g):
    xf = x.astype(jnp.float32)
    y = xf * lax.rsqrt(jnp.mean(xf * xf, axis=-1, keepdims=True) + RMS_EPS)
    return (y * g.astype(jnp.float32)).astype(x.dtype)


def dilated_band_attention(q, k, v, window, dilation, slopes):
    B, S, H, Dh = q.shape
    d = dilation
    blk = window // dilation
    L = S // d
    nb = -(-L // blk)
    lp = nb * blk

    def to_res(t):
        t = t.astype(jnp.float32).reshape(B, L, d, H, Dh).transpose(0, 3, 2, 1, 4)
        t = jnp.pad(t, ((0, 0), (0, 0), (0, 0), (0, lp - L), (0, 0)))
        return t.reshape(B, H, d, nb, blk, Dh)

    def with_prev(t):
        prev = jnp.pad(t[:, :, :, :-1], ((0, 0), (0, 0), (0, 0), (1, 0), (0, 0), (0, 0)))
        return jnp.concatenate([prev, t], axis=4)

    qb = to_res(q)
    kk = with_prev(to_res(k))
    vv = with_prev(to_res(v))
    s = jnp.einsum('bhrnqd,bhrnkd->bhrnqk', qb, kk) * (Dh ** -0.5)
    qi = jnp.arange(blk)[:, None]
    kj = jnp.arange(2 * blk)[None, :]
    rel = blk + qi - kj
    blk_id = jnp.arange(nb)[:, None, None]
    valid = (rel >= 0) & (rel <= blk) & ((blk_id > 0) | (kj >= blk))
    bias = -slopes[:, None, None] * (rel * d).astype(jnp.float32)
    s = jnp.where(valid[None, None, None], s + bias[None, :, None, None], NEG_INF)
    m = jnp.max(s, axis=-1, keepdims=True)
    p = jnp.exp(s - m)
    den = jnp.sum(p, axis=-1)
    o = jnp.einsum('bhrnqk,bhrnkd->bhrnqd', p, vv) / den[..., None]
    lse = m[..., 0] + jnp.log(den)
    o = o.reshape(B, H, d, lp, Dh)[:, :, :, :L].transpose(0, 3, 2, 1, 4).reshape(B, S, H, Dh)
    lse = lse.reshape(B, H, d, lp)[:, :, :, :L].transpose(0, 3, 2, 1).reshape(B, S, H)
    return o, lse


def moba_attention(q, k, v, slopes):
    B, S, H, Dh = q.shape
    blk = MOBA_BLOCK
    nb = -(-S // blk)
    sp = nb * blk
    qc = MOBA_QCHUNK
    nch = sp // qc
    topk = min(MOBA_TOPK, nb)
    scale = Dh ** -0.5

    def prep(t):
        t = jnp.pad(t.astype(jnp.float32), ((0, 0), (0, sp - S), (0, 0), (0, 0)))
        return t.transpose(0, 2, 1, 3)

    qp, kp, vp = prep(q), prep(k), prep(v)
    kb = kp.reshape(B, H, nb, blk, Dh)
    vb = vp.reshape(B, H, nb, blk, Dh)
    kmean = jnp.mean(kb, axis=3)
    gate = jnp.einsum('bhtd,bhnd->bhtn', qp, kmean)
    qblk = jnp.arange(sp) // blk
    past = jnp.arange(nb)[None, :] < qblk[:, None]
    gate = jnp.where(past, gate, NEG_INF)
    _, sel = lax.top_k(gate, topk)
    sel_ok = sel < qblk[None, None, :, None]

    def chunks(t):
        return jnp.moveaxis(t.reshape(B, H, nch, qc, *t.shape[3:]), 2, 0)

    bidx = jnp.arange(B)[:, None, None, None]
    hidx = jnp.arange(H)[None, :, None, None]
    m = slopes[None, :, None, None]
    koff = jnp.arange(blk)

    def one_chunk(args):
        q_c, sel_c, ok_c, ci = args
        t_pos = ci * qc + jnp.arange(qc)
        c = (ci * qc) // blk
        k_sel = kb[bidx, hidx, sel_c]
        v_sel = vb[bidx, hidx, sel_c]
        s_pos = sel_c[..., None] * blk + koff
        d_sel = (t_pos[:, None, None] - s_pos).astype(jnp.float32)
        s_sel = jnp.einsum('bhqd,bhqnkd->bhqnk', q_c, k_sel) * scale - m[..., None] * d_sel
        s_sel = jnp.where(ok_c[..., None], s_sel, NEG_INF)
        k_own = lax.dynamic_index_in_dim(kb, c, axis=2, keepdims=False)
        v_own = lax.dynamic_index_in_dim(vb, c, axis=2, keepdims=False)
        d_own = t_pos[:, None] - (c * blk + koff)[None, :]
        s_own = jnp.einsum('bhqd,bhkd->bhqk', q_c, k_own) * scale - m * d_own.astype(jnp.float32)
        s_own = jnp.where(d_own >= 0, s_own, NEG_INF)
        s_all = jnp.concatenate([s_sel.reshape(B, H, qc, topk * blk), s_own], axis=-1)
        p = jax.nn.softmax(s_all, axis=-1)
        p_sel = p[..., :topk * blk].reshape(B, H, qc, topk, blk)
        p_own = p[..., topk * blk:]
        return (jnp.einsum('bhqnk,bhqnkd->bhqd', p_sel, v_sel)
                + jnp.einsum('bhqk,bhkd->bhqd', p_own, v_own))

    out = lax.map(one_chunk, (chunks(qp), chunks(sel), chunks(sel_ok), jnp.arange(nch)))
    out = jnp.moveaxis(out, 0, 2).reshape(B, H, sp, Dh)[:, :, :S]
    return out.transpose(0, 2, 1, 3)


def hybrid_layer(x, norm_g, w_in, b_merge, w_pa, w_pb, w_out):
    B, S, _ = x.shape
    h = rmsnorm(x, norm_g)
    proj = jnp.einsum('bsd,dc->bsc', h, w_in)
    offs = [int(o) for o in np.cumsum(SPLIT_SIZES)[:-1]]
    a_qkv, z_a, b_qkv, z_b, r_a, r_b = jnp.split(proj, offs, axis=-1)

    a_qkv = a_qkv.reshape(B, S, N_DIL_GROUPS, 3, DIL_HEADS, HEAD_DIM)
    slopes_a = alibi_slopes(N_DIL_GROUPS * DIL_HEADS).reshape(N_DIL_GROUPS, DIL_HEADS)
    outs, lses = [], []
    for g, (window, dilation) in enumerate(DIL_CONFIGS):
        o, lse = dilated_band_attention(a_qkv[:, :, g, 0], a_qkv[:, :, g, 1], a_qkv[:, :, g, 2],
                                        window, dilation, slopes_a[g])
        outs.append(o)
        lses.append(lse)
    wts = jax.nn.softmax(jnp.stack(lses, axis=0), axis=0)
    attn_a = jnp.sum(wts[..., None] * jnp.stack(outs, axis=0), axis=0).reshape(B, S, DIL_WIDTH)
    y_a = jnp.einsum('bsc,cd->bsd', attn_a.astype(x.dtype) * jax.nn.silu(z_a), w_pa)

    b_qkv = b_qkv.reshape(B, S, 3, MOBA_HEADS, HEAD_DIM)
    attn_b = moba_attention(b_qkv[:, :, 0], b_qkv[:, :, 1], b_qkv[:, :, 2],
                            alibi_slopes(MOBA_HEADS)).reshape(B, S, MOBA_WIDTH)
    y_b = jnp.einsum('bsc,cd->bsd', attn_b.astype(x.dtype) * jax.nn.silu(z_b), w_pb)

    merged = (jax.nn.sigmoid(r_a + b_merge[:D_MODEL]) * y_a
              + jax.nn.sigmoid(r_b + b_merge[D_MODEL:]) * y_b)
    return x + jnp.einsum('bsd,de->bse', merged, w_out)


def setup_inputs(seed: int = 0) -> dict:
    key = jax.random.key(seed)
    ks = jax.random.split(key, 8)
    x = jax.random.normal(ks[0], (BATCH, SEQ, D_MODEL), jnp.float32)
    norm_g = 1.0 + 0.05 * jax.random.normal(ks[1], (DEPTH, D_MODEL), jnp.float32)
    w_in = jax.random.normal(ks[2], (DEPTH, D_MODEL, IN_COLS), jnp.float32) * D_MODEL ** -0.5
    b_merge = 0.1 * jax.random.normal(ks[3], (DEPTH, 2 * D_MODEL), jnp.float32)
    w_pa = jax.random.normal(ks[4], (DEPTH, DIL_WIDTH, D_MODEL), jnp.float32) * DIL_WIDTH ** -0.5
    w_pb = jax.random.normal(ks[5], (DEPTH, MOBA_WIDTH, D_MODEL), jnp.float32) * MOBA_WIDTH ** -0.5
    w_out = jax.random.normal(ks[6], (DEPTH, D_MODEL, D_MODEL), jnp.float32) * D_MODEL ** -0.5
    final_g = 1.0 + 0.05 * jax.random.normal(ks[7], (D_MODEL,), jnp.float32)
    return {'x': x, 'norm_g': norm_g, 'w_in': w_in, 'b_merge': b_merge,
            'w_pa': w_pa, 'w_pb': w_pb, 'w_out': w_out, 'final_g': final_g}


def reference(x, norm_g, w_in, b_merge, w_pa, w_pb, w_out, final_g):
    for layer in range(DEPTH):
        x = hybrid_layer(x, norm_g[layer], w_in[layer], b_merge[layer],
                         w_pa[layer], w_pb[layer], w_out[layer])
    return rmsnorm(x, final_g)
```

```python
import functools

import numpy as np
import jax
import jax.numpy as jnp
from jax import lax
from jax.experimental import pallas as pl
from jax.experimental.pallas import tpu as pltpu

D_MODEL = 1024
SEQ = 2048
DEPTH = 2
HEAD_DIM = 64
DIL_CONFIGS = ((128, 1), (512, 4), (2048, 16))
N_DIL_GROUPS = 3
DIL_HEADS = 4
DIL_WIDTH = DIL_HEADS * HEAD_DIM
DIL_BLK = 128
MOBA_HEADS = 8
MOBA_WIDTH = MOBA_HEADS * HEAD_DIM
MOBA_BLOCK = 256
MOBA_NB = SEQ // MOBA_BLOCK
MOBA_TOPK = 3
RMS_EPS = 1e-6
NEG_INF = -1e30
QK_SCALE = HEAD_DIM ** -0.5

LANES = 128
A_QKV = 3 * DIL_WIDTH
B_QKV = 3 * MOBA_WIDTH
QKV_COLS = N_DIL_GROUPS * A_QKV + B_QKV
ZR_COLS = DIL_WIDTH + MOBA_WIDTH + 2 * D_MODEL
VMEM_LIMIT = 56 * 1024 * 1024

F32 = jnp.float32
BF16 = jnp.bfloat16


def _slopes(n):
    return [float(np.float32(2.0 ** (-8.0 * (i + 1) / n))) for i in range(n)]


DIL_SLOPES = _slopes(N_DIL_GROUPS * DIL_HEADS)
MOBA_SLOPES = _slopes(MOBA_HEADS)


def _nt(a, b):
    return lax.dot_general(a, b, (((1,), (1,)), ((), ())), preferred_element_type=F32)


def _nn(a, b):
    return jnp.dot(a, b, preferred_element_type=F32)


def _rms(x, g):
    ms = jnp.mean(x * x, axis=-1, keepdims=True)
    return x * lax.rsqrt(ms + RMS_EPS) * g


def _head_masks():
    lane = lax.broadcasted_iota(jnp.int32, (1, LANES), 1)
    return lane < HEAD_DIM, lane >= HEAD_DIM


def _head_masks_bf16():
    m0, m1 = _head_masks()
    return jnp.where(m0, 1.0, 0.0).astype(BF16), jnp.where(m1, 1.0, 0.0).astype(BF16)


def _inproj_kernel(x_ref, g_ref, w_ref, a1_ref, a2_ref, a3_ref, b_ref):
    h = _rms(x_ref[...], g_ref[...]).astype(BF16)
    col = 0
    for o_ref in (a1_ref, a2_ref, a3_ref, b_ref):
        width = o_ref.shape[1]
        for c0 in range(0, width, A_QKV):
            o_ref[:, c0:c0 + A_QKV] = _nn(h, w_ref[:, col + c0:col + c0 + A_QKV]).astype(BF16)
        col += width


def _inproj(x2d, g, w_qkv, tm=512):
    m = x2d.shape[0]
    out_shapes = [jax.ShapeDtypeStruct((m, A_QKV), BF16)] * 3 + [jax.ShapeDtypeStruct((m, B_QKV), BF16)]
    out_specs = [pl.BlockSpec((tm, A_QKV), lambda i: (i, 0))] * 3 + [pl.BlockSpec((tm, B_QKV), lambda i: (i, 0))]
    return pl.pallas_call(
        _inproj_kernel,
        grid=(m // tm,),
        in_specs=[pl.BlockSpec((tm, D_MODEL), lambda i: (i, 0)),
                  pl.BlockSpec((1, D_MODEL), lambda i: (0, 0)),
                  pl.BlockSpec((D_MODEL, QKV_COLS), lambda i: (0, 0))],
        out_specs=out_specs,
        out_shape=out_shapes,
        compiler_params=pltpu.CompilerParams(dimension_semantics=("arbitrary",), vmem_limit_bytes=VMEM_LIMIT),
        name="inproj",
    )(x2d, g, w_qkv)


def _band_tile(q, kc, vc, kp, vp, bc_ref, bp_ref, g, prev_pen):
    m0, _ = _head_masks()
    hms = _head_masks_bf16()
    o_pairs, l_pairs = [], []
    for p in range(2):
        sl = slice(p * LANES, (p + 1) * LANES)
        o_h, l_h = [], []
        for h in range(2):
            hm, other = hms[h], hms[1 - h]
            idx = g * DIL_HEADS + p * 2 + h
            qh = q[:, sl] * hm
            s_c = _nt(qh, kc[:, sl]) + bc_ref[idx]
            m = jnp.max(s_c, axis=-1, keepdims=True)
            if kp is not None:
                s_p = _nt(qh, kp[:, sl]) + (bp_ref[idx] + prev_pen)
                m = jnp.maximum(m, jnp.max(s_p, axis=-1, keepdims=True))
            pv = _nn(jnp.exp(s_c - m).astype(BF16), vc[:, sl] * hm + other)
            if kp is not None:
                pv = pv + _nn(jnp.exp(s_p - m).astype(BF16), vp[:, sl] * hm + other)
            den = pltpu.roll(pv, HEAD_DIM, 1)
            o_h.append(pv / den)
            l_h.append(m + jnp.log(den))
        o_pairs.append(jnp.where(m0, o_h[0], o_h[1]))
        l_pairs.append(jnp.where(m0, l_h[0], l_h[1]))
    return jnp.concatenate(o_pairs, axis=1), jnp.concatenate(l_pairs, axis=1)


def _dilated_kernel(a1_ref, a2_ref, a3_ref, o1_ref, o2_ref, o3_ref, l1_ref, l2_ref, l3_ref,
                    bc_ref, bp_ref):
    b = pl.program_id(0)
    u = pl.program_id(1)

    @pl.when((b == 0) & (u == 0))
    def _():
        qi = lax.broadcasted_iota(jnp.int32, (DIL_BLK, DIL_BLK), 0)
        kj = lax.broadcasted_iota(jnp.int32, (DIL_BLK, DIL_BLK), 1)
        rel_c = qi - kj
        rel_p = DIL_BLK + qi - kj
        for g, (_, d) in enumerate(DIL_CONFIGS):
            for h in range(DIL_HEADS):
                idx = g * DIL_HEADS + h
                sl = DIL_SLOPES[idx]
                bc_ref[idx] = jnp.where(rel_c >= 0, -sl * (rel_c * d).astype(F32), NEG_INF)
                bp_ref[idx] = jnp.where(rel_p <= DIL_BLK, -sl * (rel_p * d).astype(F32), NEG_INF)

    scale = jnp.asarray(QK_SCALE, BF16)

    def rows(ref, n, c0):
        return ref[0, pl.ds(pl.multiple_of(n * DIL_BLK, DIL_BLK), DIL_BLK), c0:c0 + DIL_WIDTH]

    def banded(ref, n, g):
        n_prev = jnp.maximum(n - 1, 0)
        pen = jnp.where(n > 0, 0.0, NEG_INF).astype(F32)
        return _band_tile(rows(ref, n, 0) * scale, rows(ref, n, DIL_WIDTH), rows(ref, n, 2 * DIL_WIDTH),
                          rows(ref, n_prev, DIL_WIDTH), rows(ref, n_prev, 2 * DIL_WIDTH),
                          bc_ref, bp_ref, g, pen)

    o, l = banded(a1_ref, u, 0)
    o1_ref[0] = o.astype(BF16)
    l1_ref[0] = l
    o, l = banded(a2_ref, u % 4, 1)
    o2_ref[0] = o.astype(BF16)
    l2_ref[0] = l
    o, l = _band_tile(a3_ref[0, :, 0:DIL_WIDTH] * scale, a3_ref[0, :, DIL_WIDTH:2 * DIL_WIDTH],
                      a3_ref[0, :, 2 * DIL_WIDTH:], None, None, bc_ref, bp_ref, 2, None)
    o3_ref[0] = o.astype(BF16)
    l3_ref[0] = l


def _dilated(a1, a2, a3, batch):
    views = []
    for a, (_, d) in zip((a1, a2, a3), DIL_CONFIGS):
        views.append(a.reshape(batch, SEQ // d, d * A_QKV))
    nres = SEQ // DIL_BLK
    in_specs = [
        pl.BlockSpec((1, SEQ, A_QKV), lambda b, u: (b, 0, 0)),
        pl.BlockSpec((1, SEQ // 4, A_QKV), lambda b, u: (b, 0, u // 4)),
        pl.BlockSpec((1, SEQ // 16, A_QKV), lambda b, u: (b, 0, u)),
    ]
    out_maps = [lambda b, u: (b, u, 0), lambda b, u: (b, u % 4, u // 4), lambda b, u: (b, 0, u)]
    out_specs = [pl.BlockSpec((1, DIL_BLK, DIL_WIDTH), mp) for mp in out_maps] * 2
    out_shape = ([jax.ShapeDtypeStruct((batch, SEQ // d, d * DIL_WIDTH), BF16) for _, d in DIL_CONFIGS]
                 + [jax.ShapeDtypeStruct((batch, SEQ // d, d * DIL_WIDTH), F32) for _, d in DIL_CONFIGS])
    outs = pl.pallas_call(
        _dilated_kernel,
        grid=(batch, nres),
        in_specs=in_specs,
        out_specs=out_specs,
        out_shape=out_shape,
        scratch_shapes=[pltpu.VMEM((N_DIL_GROUPS * DIL_HEADS, DIL_BLK, DIL_BLK), F32),
                        pltpu.VMEM((N_DIL_GROUPS * DIL_HEADS, DIL_BLK, DIL_BLK), F32)],
        compiler_params=pltpu.CompilerParams(dimension_semantics=("arbitrary", "arbitrary"),
                                             vmem_limit_bytes=VMEM_LIMIT),
        name="dilated",
    )(*views)
    return [o.reshape(batch * SEQ, DIL_WIDTH) for o in outs]


def _moba_kernel(q_ref, k_ref, v_ref, o_ref, km_ref, bd_ref, bp_ref, acc_ref, m_ref):
    b = pl.program_id(0)
    i = pl.program_id(1)
    blk = MOBA_BLOCK

    @pl.when((b == 0) & (i == 0))
    def _():
        qi = lax.broadcasted_iota(jnp.int32, (blk, blk), 0)
        kj = lax.broadcasted_iota(jnp.int32, (blk, blk), 1)
        rel = (qi - kj).astype(F32)
        for h in range(MOBA_HEADS):
            bd_ref[h] = jnp.where(qi >= kj, -MOBA_SLOPES[h] * rel, NEG_INF)
            bp_ref[h] = -MOBA_SLOPES[h] * rel

    @pl.when(i == 0)
    def _():
        for j in range(MOBA_NB):
            kb = k_ref[0, j * blk:(j + 1) * blk, :].astype(F32)
            km_ref[j:j + 1, :] = jnp.mean(kb, axis=0, keepdims=True)

    m0, m1 = _head_masks()
    hms = _head_masks_bf16()
    lane = lax.broadcasted_iota(jnp.int32, (1, LANES), 1)
    scale = jnp.asarray(QK_SCALE, BF16)
    eye = jnp.where(lax.broadcasted_iota(jnp.int32, (blk, blk), 0)
                    == lax.broadcasted_iota(jnp.int32, (blk, blk), 1), 1.0, 0.0).astype(BF16)
    ridx = lax.broadcasted_iota(jnp.int32, (MOBA_NB, blk), 0)
    row0 = pl.multiple_of(i * blk, blk)

    for p in range(MOBA_HEADS // 2):
        sl = slice(p * LANES, (p + 1) * LANES)
        q_raw = q_ref[0, :, sl]
        q_s = q_raw * scale
        km = km_ref[:, sl]
        k_own = k_ref[0, pl.ds(row0, blk), sl]
        v_own = v_ref[0, pl.ds(row0, blk), sl]
        q_aug = []
        for h in range(2):
            hm, other = hms[h], hms[1 - h]
            head = 2 * p + h
            km_h = jnp.where((m0, m1)[h], km, 0.0)
            km_hi = km_h.astype(BF16)
            km_lo = (km_h - km_hi.astype(F32)).astype(BF16)
            gate = _nt(km_hi, q_raw) + _nt(km_lo, q_raw)
            gate = jnp.where(ridx < i, gate, NEG_INF)
            cnt = jnp.zeros((MOBA_NB, blk), jnp.int32)
            for jp in range(MOBA_NB):
                gj = gate[jp:jp + 1, :]
                beats = (gj > gate) | ((gj == gate) & (jp < ridx))
                cnt = cnt + beats.astype(jnp.int32)
            sel = (cnt < MOBA_TOPK) & (ridx < i)
            selb = jnp.where(sel, 0.0, NEG_INF)
            base = HEAD_DIM if h == 0 else 0
            pieces = []
            if base:
                pieces.append(jnp.zeros((base, blk), F32))
            pieces.append(selb)
            pieces.append(jnp.zeros((LANES - base - MOBA_NB, blk), F32))
            placed = jnp.concatenate(pieces, axis=0).astype(BF16)
            selb_rows = _nt(eye, placed)
            q_aug.append(q_s * hm + selb_rows.astype(BF16))

            s = _nt(q_s * hm, k_own) + bd_ref[head]
            m = jnp.max(s, axis=-1, keepdims=True)
            m_ref[h] = m
            acc_ref[h] = _nn(jnp.exp(s - m).astype(BF16), v_own * hm + other)

        def past_block(j, carry):
            r0 = pl.multiple_of(j * blk, blk)
            kb = k_ref[0, pl.ds(r0, blk), sl]
            vb = v_ref[0, pl.ds(r0, blk), sl]
            dist = jnp.full((1, blk), (i - j) * blk, jnp.int32).astype(F32)
            for h in range(2):
                hm, other = hms[h], hms[1 - h]
                head = 2 * p + h
                base = HEAD_DIM if h == 0 else 0
                k_aug = kb * hm + jnp.where(lane == base + j, 1.0, 0.0).astype(BF16)
                s = _nt(q_aug[h], k_aug) + (bp_ref[head] - MOBA_SLOPES[head] * dist)
                m_old = m_ref[h]
                m_new = jnp.maximum(m_old, jnp.max(s, axis=-1, keepdims=True))
                alpha = jnp.exp(m_old - m_new)
                m_ref[h] = m_new
                acc_ref[h] = alpha * acc_ref[h] + _nn(jnp.exp(s - m_new).astype(BF16),
                                                      vb * hm + other)
            return carry

        lax.fori_loop(0, i, past_block, 0)

        outs = []
        for h in range(2):
            acc = acc_ref[h]
            outs.append(acc / pltpu.roll(acc, HEAD_DIM, 1))
        o_ref[0, :, sl] = jnp.where(m0, outs[0], outs[1]).astype(BF16)


def _moba(qkv_b, batch):
    v3 = qkv_b.reshape(batch, SEQ, B_QKV)
    blk = MOBA_BLOCK
    out = pl.pallas_call(
        _moba_kernel,
        grid=(batch, MOBA_NB),
        in_specs=[pl.BlockSpec((1, blk, MOBA_WIDTH), lambda b, i: (b, i, 0)),
                  pl.BlockSpec((1, SEQ, MOBA_WIDTH), lambda b, i: (b, 0, 1)),
                  pl.BlockSpec((1, SEQ, MOBA_WIDTH), lambda b, i: (b, 0, 2))],
        out_specs=pl.BlockSpec((1, blk, MOBA_WIDTH), lambda b, i: (b, i, 0)),
        out_shape=jax.ShapeDtypeStruct((batch, SEQ, MOBA_WIDTH), BF16),
        scratch_shapes=[pltpu.VMEM((MOBA_NB, MOBA_WIDTH), F32),
                        pltpu.VMEM((MOBA_HEADS, blk, blk), F32),
                        pltpu.VMEM((MOBA_HEADS, blk, blk), F32),
                        pltpu.VMEM((2, blk, LANES), F32),
                        pltpu.VMEM((2, blk, 1), F32)],
        compiler_params=pltpu.CompilerParams(dimension_semantics=("arbitrary", "arbitrary"),
                                             vmem_limit_bytes=VMEM_LIMIT),
        name="moba",
    )(v3, v3, v3)
    return out.reshape(batch * SEQ, MOBA_WIDTH)


def _sigmoid(x):
    return 1.0 / (1.0 + jnp.exp(-x))


def _post_kernel(x_ref, o1_ref, o2_ref, o3_ref, l1_ref, l2_ref, l3_ref, ab_ref, g_ref, wzr_ref,
                 bm_ref, wpa_ref, wpb_ref, wout_ref, fg_ref, out_ref, *, final):
    x = x_ref[...]
    h = _rms(x, g_ref[...]).astype(BF16)

    l1, l2, l3 = l1_ref[...], l2_ref[...], l3_ref[...]
    lmax = jnp.maximum(jnp.maximum(l1, l2), l3)
    w1, w2, w3 = jnp.exp(l1 - lmax), jnp.exp(l2 - lmax), jnp.exp(l3 - lmax)
    attn_a = (w1 * o1_ref[...].astype(F32) + w2 * o2_ref[...].astype(F32)
              + w3 * o3_ref[...].astype(F32)) / (w1 + w2 + w3)
    c = 0
    z_a = _nn(h, wzr_ref[:, c:c + DIL_WIDTH])
    c += DIL_WIDTH
    y_a = _nn((attn_a * (z_a * _sigmoid(z_a))).astype(BF16), wpa_ref[...])
    z_b = _nn(h, wzr_ref[:, c:c + MOBA_WIDTH])
    c += MOBA_WIDTH
    y_b = _nn((ab_ref[...].astype(F32) * (z_b * _sigmoid(z_b))).astype(BF16), wpb_ref[...])
    r_a = _nn(h, wzr_ref[:, c:c + D_MODEL]) + bm_ref[:, 0:D_MODEL]
    c += D_MODEL
    merged = _sigmoid(r_a) * y_a
    r_b = _nn(h, wzr_ref[:, c:c + D_MODEL]) + bm_ref[:, D_MODEL:]
    merged = merged + _sigmoid(r_b) * y_b
    xn = x + _nn(merged.astype(BF16), wout_ref[...])
    if final:
        xn = _rms(xn, fg_ref[...])
    out_ref[...] = xn


def _post(x2d, o, l, ab, g, w_zr, bm, w_pa, w_pb, w_out, fg, final, tm=256):
    m = x2d.shape[0]
    row = lambda w: pl.BlockSpec((tm, w), lambda i: (i, 0))
    full = lambda a: pl.BlockSpec(a.shape, lambda i: (0, 0))
    return pl.pallas_call(
        functools.partial(_post_kernel, final=final),
        grid=(m // tm,),
        in_specs=[row(D_MODEL)] + [row(DIL_WIDTH)] * 6 + [row(MOBA_WIDTH), full(g), full(w_zr), full(bm),
                                                           full(w_pa), full(w_pb), full(w_out), full(fg)],
        out_specs=row(D_MODEL),
        out_shape=jax.ShapeDtypeStruct((m, D_MODEL), F32),
        compiler_params=pltpu.CompilerParams(dimension_semantics=("arbitrary",), vmem_limit_bytes=VMEM_LIMIT),
        name="post",
    )(x2d, *o, *l, ab, g, w_zr, bm, w_pa, w_pb, w_out, fg)


@jax.jit
def kernel(x, norm_g, w_in, b_merge, w_pa, w_pb, w_out, final_g):
    batch, seq, d = x.shape
    assert (seq, d) == (SEQ, D_MODEL)
    x2d = x.reshape(batch * seq, d)
    a_cols = N_DIL_GROUPS * A_QKV
    zb0 = a_cols + DIL_WIDTH
    zb1 = zb0 + B_QKV
    for layer in range(DEPTH):
        w = w_in[layer]
        w_qkv = jnp.concatenate([w[:, :a_cols], w[:, zb0:zb1]], axis=1).astype(BF16)
        w_zr = jnp.concatenate([w[:, a_cols:zb0], w[:, zb1:]], axis=1).astype(BF16)
        a1, a2, a3, qkv_b = _inproj(x2d, norm_g[layer][None, :], w_qkv)
        outs = _dilated(a1, a2, a3, batch)
        ab = _moba(qkv_b, batch)
        x2d = _post(x2d, outs[:3], outs[3:], ab, norm_g[layer][None, :], w_zr, b_merge[layer][None, :],
                    w_pa[layer].astype(BF16), w_pb[layer].astype(BF16), w_out[layer].astype(BF16),
                    final_g[None, :], final=(layer == DEPTH - 1))
    return x2d.reshape(batch, seq, d)
```

```python
import functools

import numpy as np
import jax
import jax.numpy as jnp
from jax import lax
from jax.experimental import pallas as pl
from jax.experimental.pallas import tpu as pltpu

D_MODEL = 1024
SEQ = 2048
DEPTH = 2
HEAD_DIM = 64
DIL_CONFIGS = ((128, 1), (512, 4), (2048, 16))
N_DIL_GROUPS = 3
DIL_HEADS = 4
DIL_WIDTH = DIL_HEADS * HEAD_DIM
DIL_BLK = 128
MOBA_HEADS = 8
MOBA_WIDTH = MOBA_HEADS * HEAD_DIM
MOBA_BLOCK = 256
MOBA_NB = SEQ // MOBA_BLOCK
MOBA_TOPK = 3
RMS_EPS = 1e-6
NEG_INF = -1e30
QK_SCALE = HEAD_DIM ** -0.5

LANES = 128
A_QKV = 3 * DIL_WIDTH
B_QKV = 3 * MOBA_WIDTH
QKV_COLS = N_DIL_GROUPS * A_QKV + B_QKV
ZR_COLS = DIL_WIDTH + MOBA_WIDTH + 2 * D_MODEL
VMEM_LIMIT = 56 * 1024 * 1024

F32 = jnp.float32
BF16 = jnp.bfloat16


def _slopes(n):
    return [float(np.float32(2.0 ** (-8.0 * (i + 1) / n))) for i in range(n)]


DIL_SLOPES = _slopes(N_DIL_GROUPS * DIL_HEADS)
MOBA_SLOPES = _slopes(MOBA_HEADS)


def _nt(a, b):
    return lax.dot_general(a, b, (((1,), (1,)), ((), ())), preferred_element_type=F32)


def _nn(a, b):
    return jnp.dot(a, b, preferred_element_type=F32)


def _rms(x, g):
    ms = jnp.mean(x * x, axis=-1, keepdims=True)
    return x * lax.rsqrt(ms + RMS_EPS) * g


def _head_masks():
    lane = lax.broadcasted_iota(jnp.int32, (1, LANES), 1)
    return lane < HEAD_DIM, lane >= HEAD_DIM


def _head_masks_bf16():
    m0, m1 = _head_masks()
    return jnp.where(m0, 1.0, 0.0).astype(BF16), jnp.where(m1, 1.0, 0.0).astype(BF16)


def _inproj_kernel(x_ref, g_ref, w_ref, a1_ref, a2_ref, a3_ref, b_ref):
    h = _rms(x_ref[...], g_ref[...]).astype(BF16)
    col = 0
    for o_ref in (a1_ref, a2_ref, a3_ref, b_ref):
        width = o_ref.shape[1]
        for c0 in range(0, width, A_QKV):
            o_ref[:, c0:c0 + A_QKV] = _nn(h, w_ref[:, col + c0:col + c0 + A_QKV]).astype(BF16)
        col += width


def _inproj(x2d, g, w_qkv, tm=512):
    m = x2d.shape[0]
    out_shapes = [jax.ShapeDtypeStruct((m, A_QKV), BF16)] * 3 + [jax.ShapeDtypeStruct((m, B_QKV), BF16)]
    out_specs = [pl.BlockSpec((tm, A_QKV), lambda i: (i, 0))] * 3 + [pl.BlockSpec((tm, B_QKV), lambda i: (i, 0))]
    return pl.pallas_call(
        _inproj_kernel,
        grid=(m // tm,),
        in_specs=[pl.BlockSpec((tm, D_MODEL), lambda i: (i, 0)),
                  pl.BlockSpec((1, D_MODEL), lambda i: (0, 0)),
                  pl.BlockSpec((D_MODEL, QKV_COLS), lambda i: (0, 0))],
        out_specs=out_specs,
        out_shape=out_shapes,
        compiler_params=pltpu.CompilerParams(dimension_semantics=("arbitrary",), vmem_limit_bytes=VMEM_LIMIT),
        name="inproj",
    )(x2d, g, w_qkv)


def _band_tile(q, kc, vc, kp, vp, bc_ref, bp_ref, g, prev_pen):
    m0, _ = _head_masks()
    hms = _head_masks_bf16()
    o_pairs, l_pairs = [], []
    for p in range(2):
        sl = slice(p * LANES, (p + 1) * LANES)
        o_h, l_h = [], []
        for h in range(2):
            hm, other = hms[h], hms[1 - h]
            idx = g * DIL_HEADS + p * 2 + h
            qh = q[:, sl] * hm
            s_c = _nt(qh, kc[:, sl]) + bc_ref[idx]
            m = jnp.max(s_c, axis=-1, keepdims=True)
            if kp is not None:
                s_p = _nt(qh, kp[:, sl]) + (bp_ref[idx] + prev_pen)
                m = jnp.maximum(m, jnp.max(s_p, axis=-1, keepdims=True))
            pv = _nn(jnp.exp(s_c - m).astype(BF16), vc[:, sl] * hm + other)
            if kp is not None:
                pv = pv + _nn(jnp.exp(s_p - m).astype(BF16), vp[:, sl] * hm + other)
            den = pltpu.roll(pv, HEAD_DIM, 1)
            o_h.append(pv / den)
            l_h.append(m + jnp.log(den))
        o_pairs.append(jnp.where(m0, o_h[0], o_h[1]))
        l_pairs.append(jnp.where(m0, l_h[0], l_h[1]))
    return jnp.concatenate(o_pairs, axis=1), jnp.concatenate(l_pairs, axis=1)


def _dilated_kernel(a1_ref, a2_ref, a3_ref, o1_ref, o2_ref, o3_ref, l1_ref, l2_ref, l3_ref,
                    bc_ref, bp_ref):
    b = pl.program_id(0)
    u = pl.program_id(1)

    @pl.when((b == 0) & (u == 0))
    def _():
        qi = lax.broadcasted_iota(jnp.int32, (DIL_BLK, DIL_BLK), 0)
        kj = lax.broadcasted_iota(jnp.int32, (DIL_BLK, DIL_BLK), 1)
        rel_c = qi - kj
        rel_p = DIL_BLK + qi - kj
        for g, (_, d) in enumerate(DIL_CONFIGS):
            for h in range(DIL_HEADS):
                idx = g * DIL_HEADS + h
                sl = DIL_SLOPES[idx]
                bc_ref[idx] = jnp.where(rel_c >= 0, -sl * (rel_c * d).astype(F32), NEG_INF)
                bp_ref[idx] = jnp.where(rel_p <= DIL_BLK, -sl * (rel_p * d).astype(F32), NEG_INF)

    scale = jnp.asarray(QK_SCALE, BF16)

    def rows(ref, n, c0):
        return ref[0, pl.ds(pl.multiple_of(n * DIL_BLK, DIL_BLK), DIL_BLK), c0:c0 + DIL_WIDTH]

    def banded(ref, n, g):
        n_prev = jnp.maximum(n - 1, 0)
        pen = jnp.where(n > 0, 0.0, NEG_INF).astype(F32)
        return _band_tile(rows(ref, n, 0) * scale, rows(ref, n, DIL_WIDTH), rows(ref, n, 2 * DIL_WIDTH),
                          rows(ref, n_prev, DIL_WIDTH), rows(ref, n_prev, 2 * DIL_WIDTH),
                          bc_ref, bp_ref, g, pen)

    o, l = banded(a1_ref, u, 0)
    o1_ref[0] = o.astype(BF16)
    l1_ref[0] = l
    o, l = banded(a2_ref, u % 4, 1)
    o2_ref[0] = o.astype(BF16)
    l2_ref[0] = l
    o, l = _band_tile(a3_ref[0, :, 0:DIL_WIDTH] * scale, a3_ref[0, :, DIL_WIDTH:2 * DIL_WIDTH],
                      a3_ref[0, :, 2 * DIL_WIDTH:], None, None, bc_ref, bp_ref, 2, None)
    o3_ref[0] = o.astype(BF16)
    l3_ref[0] = l


def _dilated(a1, a2, a3, batch):
    views = []
    for a, (_, d) in zip((a1, a2, a3), DIL_CONFIGS):
        views.append(a.reshape(batch, SEQ // d, d * A_QKV))
    nres = SEQ // DIL_BLK
    in_specs = [
        pl.BlockSpec((1, SEQ, A_QKV), lambda b, u: (b, 0, 0)),
        pl.BlockSpec((1, SEQ // 4, A_QKV), lambda b, u: (b, 0, u // 4)),
        pl.BlockSpec((1, SEQ // 16, A_QKV), lambda b, u: (b, 0, u)),
    ]
    out_maps = [lambda b, u: (b, u, 0), lambda b, u: (b, u % 4, u // 4), lambda b, u: (b, 0, u)]
    out_specs = [pl.BlockSpec((1, DIL_BLK, DIL_WIDTH), mp) for mp in out_maps] * 2
    out_shape = ([jax.ShapeDtypeStruct((batch, SEQ // d, d * DIL_WIDTH), BF16) for _, d in DIL_CONFIGS]
                 + [jax.ShapeDtypeStruct((batch, SEQ // d, d * DIL_WIDTH), F32) for _, d in DIL_CONFIGS])
    outs = pl.pallas_call(
        _dilated_kernel,
        grid=(batch, nres),
        in_specs=in_specs,
        out_specs=out_specs,
        out_shape=out_shape,
        scratch_shapes=[pltpu.VMEM((N_DIL_GROUPS * DIL_HEADS, DIL_BLK, DIL_BLK), F32),
                        pltpu.VMEM((N_DIL_GROUPS * DIL_HEADS, DIL_BLK, DIL_BLK), F32)],
        compiler_params=pltpu.CompilerParams(dimension_semantics=("arbitrary", "arbitrary"),
                                             vmem_limit_bytes=VMEM_LIMIT),
        name="dilated",
    )(*views)
    return [o.reshape(batch * SEQ, DIL_WIDTH) for o in outs]


def _moba_kernel(q_ref, k_ref, v_ref, o_ref, kaug_ref, vaug_ref, gtab_ref, causal_ref):
    b = pl.program_id(0)
    p = pl.program_id(1)
    blk = MOBA_BLOCK

    @pl.when((b == 0) & (p == 0))
    def _():
        u = lax.broadcasted_iota(jnp.int32, (1, SEQ), 1) - (SEQ - blk)
        for h in range(MOBA_HEADS):
            gtab_ref[h:h + 1, :] = MOBA_SLOPES[h] * u.astype(F32)
        qi = lax.broadcasted_iota(jnp.int32, (blk, blk), 0)
        kj = lax.broadcasted_iota(jnp.int32, (blk, blk), 1)
        causal_ref[...] = jnp.where(qi >= kj, 0.0, NEG_INF)
        row_blk = lax.broadcasted_iota(jnp.int32, (SEQ, LANES), 0) // blk
        lane_id = lax.broadcasted_iota(jnp.int32, (SEQ, LANES), 1)
        kaug_ref[:, LANES:] = jnp.where(row_blk == lane_id, 1.0, 0.0).astype(BF16)

    m0, m1 = _head_masks()
    hms = _head_masks_bf16()
    scale = jnp.asarray(QK_SCALE, BF16)
    eye = jnp.where(lax.broadcasted_iota(jnp.int32, (blk, blk), 0)
                    == lax.broadcasted_iota(jnp.int32, (blk, blk), 1), 1.0, 0.0).astype(BF16)
    ridx = lax.broadcasted_iota(jnp.int32, (MOBA_NB, blk), 0)

    k_all = k_ref[0]
    kaug_ref[:, :LANES] = k_all
    v_all = v_ref[0]
    for h in range(2):
        vaug_ref[h] = v_all * hms[h] + hms[1 - h]

    km = jnp.concatenate([jnp.mean(k_all[j * blk:(j + 1) * blk].astype(F32), axis=0, keepdims=True)
                          for j in range(MOBA_NB)], axis=0)
    km_hi, km_lo = [], []
    for hm in (m0, m1):
        km_h = jnp.where(hm, km, 0.0)
        hi = km_h.astype(BF16)
        km_hi.append(hi)
        km_lo.append((km_h - hi.astype(F32)).astype(BF16))

    for i in range(MOBA_NB):
        rows = slice(i * blk, (i + 1) * blk)
        nk = (i + 1) * blk
        q_raw = q_ref[0, rows, :]
        q_s = q_raw * scale
        outs = []
        for h in range(2):
            if i > 0:
                gate = _nt(km_hi[h], q_raw) + _nt(km_lo[h], q_raw)
                gate = jnp.where(ridx < i, gate, NEG_INF)
                cnt = jnp.zeros((MOBA_NB, blk), jnp.int32)
                for jp in range(MOBA_NB):
                    beats = (gate[jp:jp + 1, :] > gate) | ((gate[jp:jp + 1, :] == gate) & (jp < ridx))
                    cnt = cnt + beats.astype(jnp.int32)
                sel = ((cnt < MOBA_TOPK) & (ridx < i)) | (ridx == i)
                selb = jnp.where(sel, 0.0, NEG_INF)
                placed = jnp.concatenate([selb, jnp.zeros((LANES - MOBA_NB, blk), F32)], axis=0)
                selb_rows = _nt(eye, placed.astype(BF16)).astype(BF16)
            else:
                selb_rows = jnp.zeros((blk, LANES), BF16)
            q_aug = jnp.concatenate([q_s * hms[h], selb_rows], axis=1)
            head = 2 * p + h
            s = _nt(q_aug, kaug_ref[0:nk, :]) + gtab_ref[pl.ds(head, 1), (MOBA_NB - 1 - i) * blk:]
            s_own = s[:, i * blk:] + causal_ref[...]
            m = jnp.max(s_own, axis=-1, keepdims=True)
            if i > 0:
                s_past = s[:, :i * blk]
                m = jnp.maximum(m, jnp.max(s_past, axis=-1, keepdims=True))
            pv = _nn(jnp.exp(s_own - m).astype(BF16), vaug_ref[h, rows, :])
            if i > 0:
                pv = pv + _nn(jnp.exp(s_past - m).astype(BF16), vaug_ref[h, 0:i * blk, :])
            outs.append(pv / pltpu.roll(pv, HEAD_DIM, 1))
        o_ref[0, rows, :] = jnp.where(m0, outs[0], outs[1]).astype(BF16)


def _moba(qkv_b, batch):
    v3 = qkv_b.reshape(batch, SEQ, B_QKV)
    npairs = MOBA_HEADS // 2
    out = pl.pallas_call(
        _moba_kernel,
        grid=(batch, npairs),
        in_specs=[pl.BlockSpec((1, SEQ, LANES), lambda b, p: (b, 0, p)),
                  pl.BlockSpec((1, SEQ, LANES), lambda b, p: (b, 0, npairs + p)),
                  pl.BlockSpec((1, SEQ, LANES), lambda b, p: (b, 0, 2 * npairs + p))],
        out_specs=pl.BlockSpec((1, SEQ, LANES), lambda b, p: (b, 0, p)),
        out_shape=jax.ShapeDtypeStruct((batch, SEQ, MOBA_WIDTH), BF16),
        scratch_shapes=[pltpu.VMEM((SEQ, 2 * LANES), BF16),
                        pltpu.VMEM((2, SEQ, LANES), BF16),
                        pltpu.VMEM((MOBA_HEADS, SEQ), F32),
                        pltpu.VMEM((MOBA_BLOCK, MOBA_BLOCK), F32)],
        compiler_params=pltpu.CompilerParams(dimension_semantics=("arbitrary", "arbitrary"),
                                             vmem_limit_bytes=VMEM_LIMIT),
        name="moba",
    )(v3, v3, v3)
    return out.reshape(batch * SEQ, MOBA_WIDTH)


def _sigmoid(x):
    return 1.0 / (1.0 + jnp.exp(-x))


def _post_kernel(x_ref, o1_ref, o2_ref, o3_ref, l1_ref, l2_ref, l3_ref, ab_ref, g_ref, wzr_ref,
                 bm_ref, wpa_ref, wpb_ref, wout_ref, fg_ref, out_ref, *, final):
    x = x_ref[...]
    h = _rms(x, g_ref[...]).astype(BF16)

    l1, l2, l3 = l1_ref[...], l2_ref[...], l3_ref[...]
    lmax = jnp.maximum(jnp.maximum(l1, l2), l3)
    w1, w2, w3 = jnp.exp(l1 - lmax), jnp.exp(l2 - lmax), jnp.exp(l3 - lmax)
    attn_a = (w1 * o1_ref[...].astype(F32) + w2 * o2_ref[...].astype(F32)
              + w3 * o3_ref[...].astype(F32)) / (w1 + w2 + w3)
    c = 0
    z_a = _nn(h, wzr_ref[:, c:c + DIL_WIDTH])
    c += DIL_WIDTH
    y_a = _nn((attn_a * (z_a * _sigmoid(z_a))).astype(BF16), wpa_ref[...])
    z_b = _nn(h, wzr_ref[:, c:c + MOBA_WIDTH])
    c += MOBA_WIDTH
    y_b = _nn((ab_ref[...].astype(F32) * (z_b * _sigmoid(z_b))).astype(BF16), wpb_ref[...])
    r_a = _nn(h, wzr_ref[:, c:c + D_MODEL]) + bm_ref[:, 0:D_MODEL]
    c += D_MODEL
    merged = _sigmoid(r_a) * y_a
    r_b = _nn(h, wzr_ref[:, c:c + D_MODEL]) + bm_ref[:, D_MODEL:]
    merged = merged + _sigmoid(r_b) * y_b
    xn = x + _nn(merged.astype(BF16), wout_ref[...])
    if final:
        xn = _rms(xn, fg_ref[...])
    out_ref[...] = xn


def _post(x2d, o, l, ab, g, w_zr, bm, w_pa, w_pb, w_out, fg, final, tm=256):
    m = x2d.shape[0]
    row = lambda w: pl.BlockSpec((tm, w), lambda i: (i, 0))
    full = lambda a: pl.BlockSpec(a.shape, lambda i: (0, 0))
    return pl.pallas_call(
        functools.partial(_post_kernel, final=final),
        grid=(m // tm,),
        in_specs=[row(D_MODEL)] + [row(DIL_WIDTH)] * 6 + [row(MOBA_WIDTH), full(g), full(w_zr), full(bm),
                                                           full(w_pa), full(w_pb), full(w_out), full(fg)],
        out_specs=row(D_MODEL),
        out_shape=jax.ShapeDtypeStruct((m, D_MODEL), F32),
        compiler_params=pltpu.CompilerParams(dimension_semantics=("arbitrary",), vmem_limit_bytes=VMEM_LIMIT),
        name="post",
    )(x2d, *o, *l, ab, g, w_zr, bm, w_pa, w_pb, w_out, fg)


@jax.jit
def kernel(x, norm_g, w_in, b_merge, w_pa, w_pb, w_out, final_g):
    batch, seq, d = x.shape
    assert (seq, d) == (SEQ, D_MODEL)
    x2d = x.reshape(batch * seq, d)
    a_cols = N_DIL_GROUPS * A_QKV
    zb0 = a_cols + DIL_WIDTH
    zb1 = zb0 + B_QKV
    for layer in range(DEPTH):
        w = w_in[layer]
        w_qkv = jnp.concatenate([w[:, :a_cols], w[:, zb0:zb1]], axis=1).astype(BF16)
        w_zr = jnp.concatenate([w[:, a_cols:zb0], w[:, zb1:]], axis=1).astype(BF16)
        a1, a2, a3, qkv_b = _inproj(x2d, norm_g[layer][None, :], w_qkv)
        outs = _dilated(a1, a2, a3, batch)
        ab = _moba(qkv_b, batch)
        x2d = _post(x2d, outs[:3], outs[3:], ab, norm_g[layer][None, :], w_zr, b_merge[layer][None, :],
                    w_pa[layer].astype(BF16), w_pb[layer].astype(BF16), w_out[layer].astype(BF16),
                    final_g[None, :], final=(layer == DEPTH - 1))
    return x2d.reshape(batch, seq, d)
```

```python
import functools

import numpy as np
import jax
import jax.numpy as jnp
from jax import lax
from jax.experimental import pallas as pl
from jax.experimental.pallas import tpu as pltpu

D_MODEL = 1024
SEQ = 2048
DEPTH = 2
HEAD_DIM = 64
DIL_CONFIGS = ((128, 1), (512, 4), (2048, 16))
DILATIONS = tuple(d for _, d in DIL_CONFIGS)
N_DIL_GROUPS = 3
DIL_HEADS = 4
DIL_WIDTH = DIL_HEADS * HEAD_DIM
DIL_BLK = 128
DIL_UNITS = SEQ // DIL_BLK
MOBA_HEADS = 8
MOBA_WIDTH = MOBA_HEADS * HEAD_DIM
MOBA_BLOCK = 256
MOBA_NB = SEQ // MOBA_BLOCK
MOBA_TOPK = 3
RMS_EPS = 1e-6
NEG_INF = -1e30
QK_SCALE = HEAD_DIM ** -0.5

LANES = 128
A_QKV = 3 * DIL_WIDTH
B_QKV = 3 * MOBA_WIDTH
QKV_COLS = N_DIL_GROUPS * A_QKV + B_QKV
ZR_COLS = DIL_WIDTH + MOBA_WIDTH + 2 * D_MODEL
ROW_TILE = 512
VMEM_LIMIT = 56 * 1024 * 1024

F32 = jnp.float32
BF16 = jnp.bfloat16


def _slopes(n):
    return [float(np.float32(2.0 ** (-8.0 * (i + 1) / n))) for i in range(n)]


DIL_SLOPES = _slopes(N_DIL_GROUPS * DIL_HEADS)
MOBA_SLOPES = _slopes(MOBA_HEADS)


def _nt(a, b):
    return lax.dot_general(a, b, (((1,), (1,)), ((), ())), preferred_element_type=F32)


def _nn(a, b):
    return jnp.dot(a, b, preferred_element_type=F32)


def _bnt(a, b):
    return lax.dot_general(a, b, (((2,), (2,)), ((0,), (0,))), preferred_element_type=F32)


def _bnn(a, b):
    return lax.dot_general(a, b, (((2,), (1,)), ((0,), (0,))), preferred_element_type=F32)


def _rms(x, g):
    ms = jnp.mean(x * x, axis=-1, keepdims=True)
    return x * lax.rsqrt(ms + RMS_EPS) * g


def _head_masks():
    lane = lax.broadcasted_iota(jnp.int32, (1, LANES), 1)
    return lane < HEAD_DIM, lane >= HEAD_DIM


def _head_masks_bf16():
    m0, m1 = _head_masks()
    return jnp.where(m0, 1.0, 0.0).astype(BF16), jnp.where(m1, 1.0, 0.0).astype(BF16)


def _inproj_kernel(x_ref, g_ref, w_ref, a1_ref, a2_ref, a3_ref, b_ref, h_ref):
    h = _rms(x_ref[...], g_ref[...])
    hb = h.astype(BF16)
    a1_ref[...] = _nn(hb, w_ref[:, 0:A_QKV]).astype(BF16)
    for c0 in range(0, B_QKV, A_QKV):
        b_ref[:, c0:c0 + A_QKV] = _nn(hb, w_ref[:, 3 * A_QKV + c0:3 * A_QKV + c0 + A_QKV]).astype(BF16)
    nchunk = D_MODEL // LANES
    for j in range(nchunk):
        h_ref[j] = h[:, j * LANES:(j + 1) * LANES]
    for gi, o_ref in ((1, a2_ref), (2, a3_ref)):
        d = DILATIONS[gi]
        n = ROW_TILE // d
        hp = jnp.concatenate(
            [jnp.concatenate([h_ref[j, pl.ds(c, n, stride=d), :] for j in range(nchunk)], axis=1)
             for c in range(d)], axis=0).astype(BF16)
        r = _nn(hp, w_ref[:, gi * A_QKV:(gi + 1) * A_QKV]).astype(BF16)
        for c in range(d):
            o_ref[c] = r[c * n:(c + 1) * n]


def _inproj(x2d, g, w_qkv, batch):
    m = x2d.shape[0]
    tm = ROW_TILE
    per_b = SEQ // tm
    out_shape = [jax.ShapeDtypeStruct((m, A_QKV), BF16)]
    out_specs = [pl.BlockSpec((tm, A_QKV), lambda i: (i, 0))]
    for d in DILATIONS[1:]:
        out_shape.append(jax.ShapeDtypeStruct((batch * d, SEQ // d, A_QKV), BF16))
        out_specs.append(pl.BlockSpec((d, tm // d, A_QKV), lambda i: (i // per_b, i % per_b, 0)))
    out_shape.append(jax.ShapeDtypeStruct((m, B_QKV), BF16))
    out_specs.append(pl.BlockSpec((tm, B_QKV), lambda i: (i, 0)))
    return pl.pallas_call(
        _inproj_kernel,
        grid=(m // tm,),
        in_specs=[pl.BlockSpec((tm, D_MODEL), lambda i: (i, 0)),
                  pl.BlockSpec((1, D_MODEL), lambda i: (0, 0)),
                  pl.BlockSpec((D_MODEL, QKV_COLS), lambda i: (0, 0))],
        out_specs=out_specs,
        out_shape=out_shape,
        scratch_shapes=[pltpu.VMEM((D_MODEL // LANES, tm, LANES), F32)],
        compiler_params=pltpu.CompilerParams(dimension_semantics=("arbitrary",), vmem_limit_bytes=VMEM_LIMIT),
        name="inproj",
    )(x2d, g, w_qkv)


def _band_attend(q, kk, vv, bias_ref, g, pair, with_prev):
    m0, _ = _head_masks()
    hms = _head_masks_bf16()
    scale = jnp.asarray(QK_SCALE, BF16)
    pvs, ms = [], []
    for h in range(2):
        idx = g * DIL_HEADS + pair * 2 + h
        bias = bias_ref[idx] if with_prev else bias_ref[idx, :, DIL_BLK:]
        s = _bnt(q * (hms[h] * scale), kk) + bias
        m = jnp.max(s, axis=-1, keepdims=True)
        pvs.append(_bnn(jnp.exp(s - m).astype(BF16), vv * hms[h] + hms[1 - h]))
        ms.append(m)
    num = jnp.where(m0, pvs[0], pvs[1])
    den = pltpu.roll(jnp.where(m0, pvs[1], pvs[0]), HEAD_DIM, 2)
    return num / den, jnp.where(m0, ms[0], ms[1]) + jnp.log(den)


def _dilated_kernel(a1_ref, a2_ref, a3_ref, o1_ref, o2_ref, o3_ref, l1_ref, l2_ref, l3_ref, bias_ref):
    @pl.when(pl.program_id(0) == 0)
    def _():
        qi = lax.broadcasted_iota(jnp.int32, (DIL_BLK, 2 * DIL_BLK), 0)
        kj = lax.broadcasted_iota(jnp.int32, (DIL_BLK, 2 * DIL_BLK), 1)
        rel = DIL_BLK + qi - kj
        valid = (rel >= 0) & (rel <= DIL_BLK)
        for g, d in enumerate(DILATIONS):
            for h in range(DIL_HEADS):
                idx = g * DIL_HEADS + h
                bias_ref[idx] = jnp.where(valid, -DIL_SLOPES[idx] * (rel * d).astype(F32), NEG_INF)

    blk = DIL_BLK
    groups = ((a1_ref, o1_ref, l1_ref, 1), (a2_ref, o2_ref, l2_ref, 4), (a3_ref, o3_ref, l3_ref, 16))
    for g, (a_ref, o_ref, l_ref, nres) in enumerate(groups):
        nb = DIL_UNITS // nres
        for pair in range(2):
            lq = slice(pair * LANES, (pair + 1) * LANES)
            lk = slice(DIL_WIDTH + pair * LANES, DIL_WIDTH + (pair + 1) * LANES)
            lv = slice(2 * DIL_WIDTH + pair * LANES, 2 * DIL_WIDTH + (pair + 1) * LANES)

            def blocks(lanes):
                return a_ref[:, :, lanes].reshape(nres, nb, blk, LANES)

            q, k, v = blocks(lq), blocks(lk), blocks(lv)
            o_first, l_first = _band_attend(q[:, 0], k[:, 0], v[:, 0], bias_ref, g, pair, False)
            o_parts, l_parts = [o_first[:, None]], [l_first[:, None]]
            if nb > 1:
                e = nres * (nb - 1)
                kk = jnp.concatenate([k[:, :-1], k[:, 1:]], axis=2).reshape(e, 2 * blk, LANES)
                vv = jnp.concatenate([v[:, :-1], v[:, 1:]], axis=2).reshape(e, 2 * blk, LANES)
                o_rest, l_rest = _band_attend(q[:, 1:].reshape(e, blk, LANES), kk, vv, bias_ref, g, pair, True)
                o_parts.append(o_rest.reshape(nres, nb - 1, blk, LANES))
                l_parts.append(l_rest.reshape(nres, nb - 1, blk, LANES))
            o = jnp.concatenate(o_parts, axis=1) if nb > 1 else o_parts[0]
            l = jnp.concatenate(l_parts, axis=1) if nb > 1 else l_parts[0]
            o_ref[:, :, lq] = o.reshape(nres, nb * blk, LANES).astype(BF16)
            l_ref[:, :, lq] = l.reshape(nres, nb * blk, LANES)


def _dilated(a1, a2, a3, batch):
    ins = [a1.reshape(batch, SEQ, A_QKV), a2, a3]
    in_specs = [pl.BlockSpec((d, SEQ // d, A_QKV), lambda b: (b, 0, 0)) for d in DILATIONS]
    out_specs = [pl.BlockSpec((d, SEQ // d, DIL_WIDTH), lambda b: (b, 0, 0)) for d in DILATIONS] * 2
    out_shape = ([jax.ShapeDtypeStruct((batch * d, SEQ // d, DIL_WIDTH), BF16) for d in DILATIONS]
                 + [jax.ShapeDtypeStruct((batch * d, SEQ // d, DIL_WIDTH), F32) for d in DILATIONS])
    return pl.pallas_call(
        _dilated_kernel,
        grid=(batch,),
        in_specs=in_specs,
        out_specs=out_specs,
        out_shape=out_shape,
        scratch_shapes=[pltpu.VMEM((N_DIL_GROUPS * DIL_HEADS, DIL_BLK, 2 * DIL_BLK), F32)],
        compiler_params=pltpu.CompilerParams(dimension_semantics=("arbitrary",), vmem_limit_bytes=VMEM_LIMIT),
        name="dilated",
    )(*ins)


def _moba_kernel(q_ref, k_ref, v_ref, o_ref, kaug_ref, vaug_ref, gtab_ref, causal_ref):
    b = pl.program_id(0)
    p = pl.program_id(1)
    blk = MOBA_BLOCK

    @pl.when((b == 0) & (p == 0))
    def _():
        u = lax.broadcasted_iota(jnp.int32, (1, SEQ), 1) - (SEQ - blk)
        for h in range(MOBA_HEADS):
            gtab_ref[h:h + 1, :] = MOBA_SLOPES[h] * u.astype(F32)
        qi = lax.broadcasted_iota(jnp.int32, (blk, blk), 0)
        kj = lax.broadcasted_iota(jnp.int32, (blk, blk), 1)
        causal_ref[...] = jnp.where(qi >= kj, 0.0, NEG_INF)
        row_blk = lax.broadcasted_iota(jnp.int32, (SEQ, LANES), 0) // blk
        lane_id = lax.broadcasted_iota(jnp.int32, (SEQ, LANES), 1)
        kaug_ref[:, LANES:] = jnp.where(row_blk == lane_id, 1.0, 0.0).astype(BF16)

    m0, m1 = _head_masks()
    hms = _head_masks_bf16()
    scale = jnp.asarray(QK_SCALE, BF16)
    eye = jnp.where(lax.broadcasted_iota(jnp.int32, (blk, blk), 0)
                    == lax.broadcasted_iota(jnp.int32, (blk, blk), 1), 1.0, 0.0).astype(BF16)
    ridx = lax.broadcasted_iota(jnp.int32, (MOBA_NB, blk), 0)

    k_all = k_ref[0]
    kaug_ref[:, :LANES] = k_all
    v_all = v_ref[0]
    for h in range(2):
        vaug_ref[h] = v_all * hms[h] + hms[1 - h]

    km = jnp.concatenate([jnp.mean(k_all[j * blk:(j + 1) * blk].astype(F32), axis=0, keepdims=True)
                          for j in range(MOBA_NB)], axis=0)
    km_hi, km_lo = [], []
    for hm in (m0, m1):
        km_h = jnp.where(hm, km, 0.0)
        hi = km_h.astype(BF16)
        km_hi.append(hi)
        km_lo.append((km_h - hi.astype(F32)).astype(BF16))

    for i in range(MOBA_NB):
        rows = slice(i * blk, (i + 1) * blk)
        nk = (i + 1) * blk
        q_raw = q_ref[0, rows, :]
        q_s = q_raw * scale
        outs = []
        for h in range(2):
            if i > 0:
                gate = _nt(km_hi[h], q_raw) + _nt(km_lo[h], q_raw)
                gate = jnp.where(ridx < i, gate, NEG_INF)
                cnt = jnp.zeros((MOBA_NB, blk), jnp.int32)
                for jp in range(MOBA_NB):
                    beats = (gate[jp:jp + 1, :] > gate) | ((gate[jp:jp + 1, :] == gate) & (jp < ridx))
                    cnt = cnt + beats.astype(jnp.int32)
                sel = ((cnt < MOBA_TOPK) & (ridx < i)) | (ridx == i)
                selb = jnp.where(sel, 0.0, NEG_INF)
                placed = jnp.concatenate([selb, jnp.zeros((LANES - MOBA_NB, blk), F32)], axis=0)
                selb_rows = _nt(eye, placed.astype(BF16)).astype(BF16)
            else:
                selb_rows = jnp.zeros((blk, LANES), BF16)
            q_aug = jnp.concatenate([q_s * hms[h], selb_rows], axis=1)
            head = 2 * p + h
            s = _nt(q_aug, kaug_ref[0:nk, :]) + gtab_ref[pl.ds(head, 1), (MOBA_NB - 1 - i) * blk:]
            s_own = s[:, i * blk:] + causal_ref[...]
            m = jnp.max(s_own, axis=-1, keepdims=True)
            if i > 0:
                s_past = s[:, :i * blk]
                m = jnp.maximum(m, jnp.max(s_past, axis=-1, keepdims=True))
            pv = _nn(jnp.exp(s_own - m).astype(BF16), vaug_ref[h, rows, :])
            if i > 0:
                pv = pv + _nn(jnp.exp(s_past - m).astype(BF16), vaug_ref[h, 0:i * blk, :])
            outs.append(pv / pltpu.roll(pv, HEAD_DIM, 1))
        o_ref[0, rows, :] = jnp.where(m0, outs[0], outs[1]).astype(BF16)


def _moba(qkv_b, batch):
    v3 = qkv_b.reshape(batch, SEQ, B_QKV)
    npairs = MOBA_HEADS // 2
    out = pl.pallas_call(
        _moba_kernel,
        grid=(batch, npairs),
        in_specs=[pl.BlockSpec((1, SEQ, LANES), lambda b, p: (b, 0, p)),
                  pl.BlockSpec((1, SEQ, LANES), lambda b, p: (b, 0, npairs + p)),
                  pl.BlockSpec((1, SEQ, LANES), lambda b, p: (b, 0, 2 * npairs + p))],
        out_specs=pl.BlockSpec((1, SEQ, LANES), lambda b, p: (b, 0, p)),
        out_shape=jax.ShapeDtypeStruct((batch, SEQ, MOBA_WIDTH), BF16),
        scratch_shapes=[pltpu.VMEM((SEQ, 2 * LANES), BF16),
                        pltpu.VMEM((2, SEQ, LANES), BF16),
                        pltpu.VMEM((MOBA_HEADS, SEQ), F32),
                        pltpu.VMEM((MOBA_BLOCK, MOBA_BLOCK), F32)],
        compiler_params=pltpu.CompilerParams(dimension_semantics=("arbitrary", "arbitrary"),
                                             vmem_limit_bytes=VMEM_LIMIT),
        name="moba",
    )(v3, v3, v3)
    return out.reshape(batch * SEQ, MOBA_WIDTH)


def _sigmoid(x):
    return 1.0 / (1.0 + jnp.exp(-x))


def _post_kernel(x_ref, o1_ref, o2_ref, o3_ref, l1_ref, l2_ref, l3_ref, ab_ref, g_ref, wzr_ref,
                 bm_ref, wpa_ref, wpb_ref, wout_ref, fg_ref, out_ref, o2s, o3s, l2s, l3s, *, final):
    x = x_ref[...]
    h = _rms(x, g_ref[...]).astype(BF16)

    def position_order(src, dst):
        d, n = src.shape[0], src.shape[1]
        for c in range(d):
            blk = src[c].astype(F32)
            for j in range(DIL_WIDTH // LANES):
                dst[j, pl.ds(c, n, stride=d), :] = blk[:, j * LANES:(j + 1) * LANES]
        return jnp.concatenate([dst[j] for j in range(DIL_WIDTH // LANES)], axis=1)

    o2, l2 = position_order(o2_ref, o2s), position_order(l2_ref, l2s)
    o3, l3 = position_order(o3_ref, o3s), position_order(l3_ref, l3s)

    l1 = l1_ref[...]
    lmax = jnp.maximum(jnp.maximum(l1, l2), l3)
    w1, w2, w3 = jnp.exp(l1 - lmax), jnp.exp(l2 - lmax), jnp.exp(l3 - lmax)
    attn_a = (w1 * o1_ref[...].astype(F32) + w2 * o2 + w3 * o3) / (w1 + w2 + w3)
    c = 0
    z_a = _nn(h, wzr_ref[:, c:c + DIL_WIDTH])
    c += DIL_WIDTH
    y_a = _nn((attn_a * (z_a * _sigmoid(z_a))).astype(BF16), wpa_ref[...])
    z_b = _nn(h, wzr_ref[:, c:c + MOBA_WIDTH])
    c += MOBA_WIDTH
    y_b = _nn((ab_ref[...].astype(F32) * (z_b * _sigmoid(z_b))).astype(BF16), wpb_ref[...])
    r_a = _nn(h, wzr_ref[:, c:c + D_MODEL]) + bm_ref[:, 0:D_MODEL]
    c += D_MODEL
    merged = _sigmoid(r_a) * y_a
    r_b = _nn(h, wzr_ref[:, c:c + D_MODEL]) + bm_ref[:, D_MODEL:]
    merged = merged + _sigmoid(r_b) * y_b
    xn = x + _nn(merged.astype(BF16), wout_ref[...])
    if final:
        xn = _rms(xn, fg_ref[...])
    out_ref[...] = xn


def _post(x2d, o, l, ab, g, w_zr, bm, w_pa, w_pb, w_out, fg, final):
    m = x2d.shape[0]
    tm = ROW_TILE
    per_b = SEQ // tm
    row = lambda w: pl.BlockSpec((tm, w), lambda i: (i, 0))
    res = lambda d: pl.BlockSpec((d, tm // d, DIL_WIDTH), lambda i: (i // per_b, i % per_b, 0))
    full = lambda a: pl.BlockSpec(a.shape, lambda i: (0, 0), pipeline_mode=pl.Buffered(1))
    grp = [row(DIL_WIDTH), res(4), res(16)]
    return pl.pallas_call(
        functools.partial(_post_kernel, final=final),
        grid=(m // tm,),
        in_specs=[row(D_MODEL)] + grp + grp + [row(MOBA_WIDTH), full(g), full(w_zr), full(bm),
                                                full(w_pa), full(w_pb), full(w_out), full(fg)],
        out_specs=row(D_MODEL),
        out_shape=jax.ShapeDtypeStruct((m, D_MODEL), F32),
        scratch_shapes=[pltpu.VMEM((DIL_WIDTH // LANES, tm, LANES), F32)] * 4,
        compiler_params=pltpu.CompilerParams(dimension_semantics=("arbitrary",), vmem_limit_bytes=VMEM_LIMIT),
        name="post",
    )(x2d, *o, *l, ab, g, w_zr, bm, w_pa, w_pb, w_out, fg)


@jax.jit
def kernel(x, norm_g, w_in, b_merge, w_pa, w_pb, w_out, final_g):
    batch, seq, d = x.shape
    assert (seq, d) == (SEQ, D_MODEL)
    x2d = x.reshape(batch * seq, d)
    a_cols = N_DIL_GROUPS * A_QKV
    zb0 = a_cols + DIL_WIDTH
    zb1 = zb0 + B_QKV
    for layer in range(DEPTH):
        w = w_in[layer]
        w_qkv = jnp.concatenate([w[:, :a_cols], w[:, zb0:zb1]], axis=1).astype(BF16)
        w_zr = jnp.concatenate([w[:, a_cols:zb0], w[:, zb1:]], axis=1).astype(BF16)
        a1, a2, a3, qkv_b = _inproj(x2d, norm_g[layer][None, :], w_qkv, batch)
        outs = _dilated(a1, a2, a3, batch)
        ab = _moba(qkv_b, batch)
        o1 = outs[0].reshape(batch * seq, DIL_WIDTH)
        l1 = outs[3].reshape(batch * seq, DIL_WIDTH)
        x2d = _post(x2d, [o1, outs[1], outs[2]], [l1, outs[4], outs[5]], ab, norm_g[layer][None, :], w_zr,
                    b_merge[layer][None, :], w_pa[layer].astype(BF16), w_pb[layer].astype(BF16),
                    w_out[layer].astype(BF16), final_g[None, :], final=(layer == DEPTH - 1))
    return x2d.reshape(batch, seq, d)
```

```python
import functools

import numpy as np
import jax
import jax.numpy as jnp
from jax import lax
from jax.experimental import pallas as pl
from jax.experimental.pallas import tpu as pltpu

D_MODEL = 1024
SEQ = 2048
DEPTH = 2
HEAD_DIM = 64
DIL_CONFIGS = ((128, 1), (512, 4), (2048, 16))
DILATIONS = tuple(d for _, d in DIL_CONFIGS)
N_DIL_GROUPS = 3
DIL_HEADS = 4
DIL_WIDTH = DIL_HEADS * HEAD_DIM
DIL_BLK = 128
DIL_UNITS = SEQ // DIL_BLK
MOBA_HEADS = 8
MOBA_WIDTH = MOBA_HEADS * HEAD_DIM
MOBA_BLOCK = 256
MOBA_NB = SEQ // MOBA_BLOCK
MOBA_TOPK = 3
RMS_EPS = 1e-6
NEG_INF = -1e30
QK_SCALE = HEAD_DIM ** -0.5

LANES = 128
A_QKV = 3 * DIL_WIDTH
B_QKV = 3 * MOBA_WIDTH
QKV_COLS = N_DIL_GROUPS * A_QKV + B_QKV
ZR_COLS = DIL_WIDTH + MOBA_WIDTH + 2 * D_MODEL
ROW_TILE = 512
VMEM_LIMIT = 56 * 1024 * 1024

F32 = jnp.float32
BF16 = jnp.bfloat16


def _slopes(n):
    return [float(np.float32(2.0 ** (-8.0 * (i + 1) / n))) for i in range(n)]


DIL_SLOPES = _slopes(N_DIL_GROUPS * DIL_HEADS)
MOBA_SLOPES = _slopes(MOBA_HEADS)


def _nt(a, b):
    return lax.dot_general(a, b, (((1,), (1,)), ((), ())), preferred_element_type=F32)


def _nn(a, b):
    return jnp.dot(a, b, preferred_element_type=F32)


def _bnt(a, b):
    return lax.dot_general(a, b, (((2,), (2,)), ((0,), (0,))), preferred_element_type=F32)


def _bnn(a, b):
    return lax.dot_general(a, b, (((2,), (1,)), ((0,), (0,))), preferred_element_type=F32)


def _rms(x, g):
    ms = jnp.mean(x * x, axis=-1, keepdims=True)
    return x * lax.rsqrt(ms + RMS_EPS) * g


def _head_masks():
    lane = lax.broadcasted_iota(jnp.int32, (1, LANES), 1)
    return lane < HEAD_DIM, lane >= HEAD_DIM


def _head_masks_bf16():
    m0, m1 = _head_masks()
    return jnp.where(m0, 1.0, 0.0).astype(BF16), jnp.where(m1, 1.0, 0.0).astype(BF16)


def _inproj_kernel(x_ref, g_ref, w_ref, a1_ref, a2_ref, a3_ref, b_ref, h_ref):
    h = _rms(x_ref[...], g_ref[...])
    hb = h.astype(BF16)
    a1_ref[...] = _nn(hb, w_ref[:, 0:A_QKV]).astype(BF16)
    for c0 in range(0, B_QKV, A_QKV):
        b_ref[:, c0:c0 + A_QKV] = _nn(hb, w_ref[:, 3 * A_QKV + c0:3 * A_QKV + c0 + A_QKV]).astype(BF16)
    nchunk = D_MODEL // LANES
    for j in range(nchunk):
        h_ref[j] = h[:, j * LANES:(j + 1) * LANES]
    for gi, o_ref in ((1, a2_ref), (2, a3_ref)):
        d = DILATIONS[gi]
        n = ROW_TILE // d
        hp = jnp.concatenate(
            [jnp.concatenate([h_ref[j, pl.ds(c, n, stride=d), :] for j in range(nchunk)], axis=1)
             for c in range(d)], axis=0).astype(BF16)
        r = _nn(hp, w_ref[:, gi * A_QKV:(gi + 1) * A_QKV]).astype(BF16)
        for c in range(d):
            o_ref[c] = r[c * n:(c + 1) * n]


def _inproj(x2d, g, w_qkv, batch):
    m = x2d.shape[0]
    tm = ROW_TILE
    per_b = SEQ // tm
    out_shape = [jax.ShapeDtypeStruct((m, A_QKV), BF16)]
    out_specs = [pl.BlockSpec((tm, A_QKV), lambda i: (i, 0))]
    for d in DILATIONS[1:]:
        out_shape.append(jax.ShapeDtypeStruct((batch * d, SEQ // d, A_QKV), BF16))
        out_specs.append(pl.BlockSpec((d, tm // d, A_QKV), lambda i: (i // per_b, i % per_b, 0)))
    out_shape.append(jax.ShapeDtypeStruct((m, B_QKV), BF16))
    out_specs.append(pl.BlockSpec((tm, B_QKV), lambda i: (i, 0)))
    return pl.pallas_call(
        _inproj_kernel,
        grid=(m // tm,),
        in_specs=[pl.BlockSpec((tm, D_MODEL), lambda i: (i, 0)),
                  pl.BlockSpec((1, D_MODEL), lambda i: (0, 0)),
                  pl.BlockSpec((D_MODEL, QKV_COLS), lambda i: (0, 0))],
        out_specs=out_specs,
        out_shape=out_shape,
        scratch_shapes=[pltpu.VMEM((D_MODEL // LANES, tm, LANES), F32)],
        compiler_params=pltpu.CompilerParams(dimension_semantics=("arbitrary",), vmem_limit_bytes=VMEM_LIMIT),
        name="inproj",
    )(x2d, g, w_qkv)


def _band_attend(q, kk, vv, bias_ref, g, pair, with_prev):
    m0, _ = _head_masks()
    hms = _head_masks_bf16()
    scale = jnp.asarray(QK_SCALE, BF16)
    pvs, ms = [], []
    for h in range(2):
        idx = g * DIL_HEADS + pair * 2 + h
        bias = bias_ref[idx] if with_prev else bias_ref[idx, :, DIL_BLK:]
        s = _bnt(q * (hms[h] * scale), kk) + bias
        m = jnp.max(s, axis=-1, keepdims=True)
        pvs.append(_bnn(jnp.exp(s - m).astype(BF16), vv * hms[h] + hms[1 - h]))
        ms.append(m)
    num = jnp.where(m0, pvs[0], pvs[1])
    den = pltpu.roll(jnp.where(m0, pvs[1], pvs[0]), HEAD_DIM, 2)
    return num / den, jnp.where(m0, ms[0], ms[1]) + jnp.log(den)


def _dilated_kernel(a1_ref, a2_ref, a3_ref, o1_ref, o2_ref, o3_ref, l1_ref, l2_ref, l3_ref, bias_ref):
    @pl.when(pl.program_id(0) == 0)
    def _():
        qi = lax.broadcasted_iota(jnp.int32, (DIL_BLK, 2 * DIL_BLK), 0)
        kj = lax.broadcasted_iota(jnp.int32, (DIL_BLK, 2 * DIL_BLK), 1)
        rel = DIL_BLK + qi - kj
        valid = (rel >= 0) & (rel <= DIL_BLK)
        for g, d in enumerate(DILATIONS):
            for h in range(DIL_HEADS):
                idx = g * DIL_HEADS + h
                bias_ref[idx] = jnp.where(valid, -DIL_SLOPES[idx] * (rel * d).astype(F32), NEG_INF)

    blk = DIL_BLK
    groups = ((a1_ref, o1_ref, l1_ref, 1), (a2_ref, o2_ref, l2_ref, 4), (a3_ref, o3_ref, l3_ref, 16))
    for g, (a_ref, o_ref, l_ref, nres) in enumerate(groups):
        nb = DIL_UNITS // nres
        for pair in range(2):
            lq = slice(pair * LANES, (pair + 1) * LANES)
            lk = slice(DIL_WIDTH + pair * LANES, DIL_WIDTH + (pair + 1) * LANES)
            lv = slice(2 * DIL_WIDTH + pair * LANES, 2 * DIL_WIDTH + (pair + 1) * LANES)

            def blocks(lanes):
                return a_ref[:, :, lanes].reshape(nres, nb, blk, LANES)

            q, k, v = blocks(lq), blocks(lk), blocks(lv)
            o_first, l_first = _band_attend(q[:, 0], k[:, 0], v[:, 0], bias_ref, g, pair, False)
            o_parts, l_parts = [o_first[:, None]], [l_first[:, None]]
            if nb > 1:
                e = nres * (nb - 1)
                kk = jnp.concatenate([k[:, :-1], k[:, 1:]], axis=2).reshape(e, 2 * blk, LANES)
                vv = jnp.concatenate([v[:, :-1], v[:, 1:]], axis=2).reshape(e, 2 * blk, LANES)
                o_rest, l_rest = _band_attend(q[:, 1:].reshape(e, blk, LANES), kk, vv, bias_ref, g, pair, True)
                o_parts.append(o_rest.reshape(nres, nb - 1, blk, LANES))
                l_parts.append(l_rest.reshape(nres, nb - 1, blk, LANES))
            o = jnp.concatenate(o_parts, axis=1) if nb > 1 else o_parts[0]
            l = jnp.concatenate(l_parts, axis=1) if nb > 1 else l_parts[0]
            o_ref[:, :, lq] = o.reshape(nres, nb * blk, LANES).astype(BF16)
            l_ref[:, :, lq] = l.reshape(nres, nb * blk, LANES)


def _dilated(a1, a2, a3, batch):
    ins = [a1.reshape(batch, SEQ, A_QKV), a2, a3]
    in_specs = [pl.BlockSpec((d, SEQ // d, A_QKV), lambda b: (b, 0, 0)) for d in DILATIONS]
    out_specs = [pl.BlockSpec((d, SEQ // d, DIL_WIDTH), lambda b: (b, 0, 0)) for d in DILATIONS] * 2
    out_shape = ([jax.ShapeDtypeStruct((batch * d, SEQ // d, DIL_WIDTH), BF16) for d in DILATIONS]
                 + [jax.ShapeDtypeStruct((batch * d, SEQ // d, DIL_WIDTH), F32) for d in DILATIONS])
    return pl.pallas_call(
        _dilated_kernel,
        grid=(batch,),
        in_specs=in_specs,
        out_specs=out_specs,
        out_shape=out_shape,
        scratch_shapes=[pltpu.VMEM((N_DIL_GROUPS * DIL_HEADS, DIL_BLK, 2 * DIL_BLK), F32)],
        compiler_params=pltpu.CompilerParams(dimension_semantics=("arbitrary",), vmem_limit_bytes=VMEM_LIMIT),
        name="dilated",
    )(*ins)


def _moba_kernel(q_ref, k_ref, v_ref, o_ref, kaug_ref, vaug_ref, qaug_ref, gtab_ref, causal_ref):
    b = pl.program_id(0)
    p = pl.program_id(1)
    blk = MOBA_BLOCK

    @pl.when((b == 0) & (p == 0))
    def _():
        u = lax.broadcasted_iota(jnp.int32, (1, SEQ), 1) - (SEQ - blk)
        for h in range(MOBA_HEADS):
            gtab_ref[h:h + 1, :] = MOBA_SLOPES[h] * u.astype(F32)
        qi = lax.broadcasted_iota(jnp.int32, (blk, blk), 0)
        kj = lax.broadcasted_iota(jnp.int32, (blk, blk), 1)
        causal_ref[...] = jnp.where(qi >= kj, 0.0, NEG_INF)
        row_blk = lax.broadcasted_iota(jnp.int32, (SEQ, LANES), 0) // blk
        lane_id = lax.broadcasted_iota(jnp.int32, (SEQ, LANES), 1)
        kaug_ref[:, LANES:] = jnp.where(row_blk == lane_id, 1.0, 0.0).astype(BF16)

    m0, m1 = _head_masks()
    hms = _head_masks_bf16()
    scale = jnp.asarray(QK_SCALE, BF16)
    eye = jnp.where(lax.broadcasted_iota(jnp.int32, (blk, blk), 0)
                    == lax.broadcasted_iota(jnp.int32, (blk, blk), 1), 1.0, 0.0).astype(BF16)
    k_all = k_ref[0]
    kaug_ref[:, :LANES] = k_all
    v_all = v_ref[0]
    for h in range(2):
        vaug_ref[h] = v_all * hms[h] + hms[1 - h]

    km = jnp.concatenate([jnp.mean(k_all[j * blk:(j + 1) * blk].astype(F32), axis=0, keepdims=True)
                          for j in range(MOBA_NB)], axis=0)

    q_all = q_ref[0]
    ridx = lax.broadcasted_iota(jnp.int32, (MOBA_NB, SEQ), 0)
    qblk = lax.broadcasted_iota(jnp.int32, (MOBA_NB, SEQ), 1) // blk
    past = ridx < qblk
    for h, hm in enumerate((m0, m1)):
        km_h = jnp.where(hm, km, 0.0)
        km_hi = km_h.astype(BF16)
        km_lo = (km_h - km_hi.astype(F32)).astype(BF16)
        gate = jnp.where(past, _nt(km_hi, q_all) + _nt(km_lo, q_all), NEG_INF)
        cnt = jnp.zeros((MOBA_NB, SEQ), jnp.int32)
        for jp in range(MOBA_NB):
            beats = (gate[jp:jp + 1, :] > gate) | ((gate[jp:jp + 1, :] == gate) & (jp < ridx))
            cnt = cnt + beats.astype(jnp.int32)
        sel = ((cnt < MOBA_TOPK) & past) | (ridx == qblk)
        selb = jnp.where(sel, 0.0, NEG_INF)
        placed = jnp.concatenate([selb, jnp.zeros((LANES - MOBA_NB, SEQ), F32)], axis=0).astype(BF16)
        q_h = q_all * (hms[h] * scale)
        for i in range(MOBA_NB):
            rows = slice(i * blk, (i + 1) * blk)
            qaug_ref[i, h * blk:(h + 1) * blk, :LANES] = q_h[rows]
            qaug_ref[i, h * blk:(h + 1) * blk, LANES:] = _nt(eye, placed[:, rows]).astype(BF16)

    def scores(i):
        nk = (i + 1) * blk
        s = _nt(qaug_ref[i], kaug_ref[0:nk, :])
        parts = []
        for h in range(2):
            sh = s[h * blk:(h + 1) * blk] + gtab_ref[pl.ds(2 * p + h, 1), (MOBA_NB - 1 - i) * blk:]
            s_own = sh[:, i * blk:] + causal_ref[...]
            m = jnp.max(s_own, axis=-1, keepdims=True)
            s_past = None
            if i > 0:
                s_past = sh[:, :i * blk]
                m = jnp.maximum(m, jnp.max(s_past, axis=-1, keepdims=True))
            parts.append((s_own, s_past, m))
        return parts

    def attend(i, parts):
        rows = slice(i * blk, (i + 1) * blk)
        outs = []
        for h, (s_own, s_past, m) in enumerate(parts):
            pv = _nn(jnp.exp(s_own - m).astype(BF16), vaug_ref[h, rows, :])
            if i > 0:
                pv = pv + _nn(jnp.exp(s_past - m).astype(BF16), vaug_ref[h, 0:i * blk, :])
            outs.append(pv / pltpu.roll(pv, HEAD_DIM, 1))
        o_ref[0, rows, :] = jnp.where(m0, outs[0], outs[1]).astype(BF16)

    parts = scores(0)
    for i in range(1, MOBA_NB):
        nxt = scores(i)
        attend(i - 1, parts)
        parts = nxt
    attend(MOBA_NB - 1, parts)


def _moba(qkv_b, batch):
    v3 = qkv_b.reshape(batch, SEQ, B_QKV)
    npairs = MOBA_HEADS // 2
    out = pl.pallas_call(
        _moba_kernel,
        grid=(batch, npairs),
        in_specs=[pl.BlockSpec((1, SEQ, LANES), lambda b, p: (b, 0, p)),
                  pl.BlockSpec((1, SEQ, LANES), lambda b, p: (b, 0, npairs + p)),
                  pl.BlockSpec((1, SEQ, LANES), lambda b, p: (b, 0, 2 * npairs + p))],
        out_specs=pl.BlockSpec((1, SEQ, LANES), lambda b, p: (b, 0, p)),
        out_shape=jax.ShapeDtypeStruct((batch, SEQ, MOBA_WIDTH), BF16),
        scratch_shapes=[pltpu.VMEM((SEQ, 2 * LANES), BF16),
                        pltpu.VMEM((2, SEQ, LANES), BF16),
                        pltpu.VMEM((MOBA_NB, 2 * MOBA_BLOCK, 2 * LANES), BF16),
                        pltpu.VMEM((MOBA_HEADS, SEQ), F32),
                        pltpu.VMEM((MOBA_BLOCK, MOBA_BLOCK), F32)],
        compiler_params=pltpu.CompilerParams(dimension_semantics=("arbitrary", "arbitrary"),
                                             vmem_limit_bytes=VMEM_LIMIT),
        name="moba",
    )(v3, v3, v3)
    return out.reshape(batch * SEQ, MOBA_WIDTH)


def _sigmoid(x):
    return 1.0 / (1.0 + jnp.exp(-x))


def _post_kernel(x_ref, o1_ref, o2_ref, o3_ref, l1_ref, l2_ref, l3_ref, ab_ref, g_ref, wzr_ref,
                 bm_ref, wpa_ref, wpb_ref, wout_ref, fg_ref, out_ref, o2s, o3s, l2s, l3s, *, final):
    x = x_ref[...]
    h = _rms(x, g_ref[...]).astype(BF16)

    def position_order(src, dst):
        d, n = src.shape[0], src.shape[1]
        for c in range(d):
            blk = src[c].astype(F32)
            for j in range(DIL_WIDTH // LANES):
                dst[j, pl.ds(c, n, stride=d), :] = blk[:, j * LANES:(j + 1) * LANES]
        return jnp.concatenate([dst[j] for j in range(DIL_WIDTH // LANES)], axis=1)

    o2, l2 = position_order(o2_ref, o2s), position_order(l2_ref, l2s)
    o3, l3 = position_order(o3_ref, o3s), position_order(l3_ref, l3s)

    l1 = l1_ref[...]
    lmax = jnp.maximum(jnp.maximum(l1, l2), l3)
    w1, w2, w3 = jnp.exp(l1 - lmax), jnp.exp(l2 - lmax), jnp.exp(l3 - lmax)
    attn_a = (w1 * o1_ref[...].astype(F32) + w2 * o2 + w3 * o3) / (w1 + w2 + w3)
    c = 0
    z_a = _nn(h, wzr_ref[:, c:c + DIL_WIDTH])
    c += DIL_WIDTH
    y_a = _nn((attn_a * (z_a * _sigmoid(z_a))).astype(BF16), wpa_ref[...])
    z_b = _nn(h, wzr_ref[:, c:c + MOBA_WIDTH])
    c += MOBA_WIDTH
    y_b = _nn((ab_ref[...].astype(F32) * (z_b * _sigmoid(z_b))).astype(BF16), wpb_ref[...])
    r_a = _nn(h, wzr_ref[:, c:c + D_MODEL]) + bm_ref[:, 0:D_MODEL]
    c += D_MODEL
    merged = _sigmoid(r_a) * y_a
    r_b = _nn(h, wzr_ref[:, c:c + D_MODEL]) + bm_ref[:, D_MODEL:]
    merged = merged + _sigmoid(r_b) * y_b
    xn = x + _nn(merged.astype(BF16), wout_ref[...])
    if final:
        xn = _rms(xn, fg_ref[...])
    out_ref[...] = xn


def _post(x2d, o, l, ab, g, w_zr, bm, w_pa, w_pb, w_out, fg, final):
    m = x2d.shape[0]
    tm = ROW_TILE
    per_b = SEQ // tm
    row = lambda w: pl.BlockSpec((tm, w), lambda i: (i, 0))
    res = lambda d: pl.BlockSpec((d, tm // d, DIL_WIDTH), lambda i: (i // per_b, i % per_b, 0))
    full = lambda a: pl.BlockSpec(a.shape, lambda i: (0, 0), pipeline_mode=pl.Buffered(1))
    grp = [row(DIL_WIDTH), res(4), res(16)]
    return pl.pallas_call(
        functools.partial(_post_kernel, final=final),
        grid=(m // tm,),
        in_specs=[row(D_MODEL)] + grp + grp + [row(MOBA_WIDTH), full(g), full(w_zr), full(bm),
                                                full(w_pa), full(w_pb), full(w_out), full(fg)],
        out_specs=row(D_MODEL),
        out_shape=jax.ShapeDtypeStruct((m, D_MODEL), F32),
        scratch_shapes=[pltpu.VMEM((DIL_WIDTH // LANES, tm, LANES), F32)] * 4,
        compiler_params=pltpu.CompilerParams(dimension_semantics=("arbitrary",), vmem_limit_bytes=VMEM_LIMIT),
        name="post",
    )(x2d, *o, *l, ab, g, w_zr, bm, w_pa, w_pb, w_out, fg)


@jax.jit
def kernel(x, norm_g, w_in, b_merge, w_pa, w_pb, w_out, final_g):
    batch, seq, d = x.shape
    assert (seq, d) == (SEQ, D_MODEL)
    x2d = x.reshape(batch * seq, d)
    a_cols = N_DIL_GROUPS * A_QKV
    zb0 = a_cols + DIL_WIDTH
    zb1 = zb0 + B_QKV
    for layer in range(DEPTH):
        w = w_in[layer]
        w_qkv = jnp.concatenate([w[:, :a_cols], w[:, zb0:zb1]], axis=1).astype(BF16)
        w_zr = jnp.concatenate([w[:, a_cols:zb0], w[:, zb1:]], axis=1).astype(BF16)
        a1, a2, a3, qkv_b = _inproj(x2d, norm_g[layer][None, :], w_qkv, batch)
        outs = _dilated(a1, a2, a3, batch)
        ab = _moba(qkv_b, batch)
        o1 = outs[0].reshape(batch * seq, DIL_WIDTH)
        l1 = outs[3].reshape(batch * seq, DIL_WIDTH)
        x2d = _post(x2d, [o1, outs[1], outs[2]], [l1, outs[4], outs[5]], ab, norm_g[layer][None, :], w_zr,
                    b_merge[layer][None, :], w_pa[layer].astype(BF16), w_pb[layer].astype(BF16),
                    w_out[layer].astype(BF16), final_g[None, :], final=(layer == DEPTH - 1))
    return x2d.reshape(batch, seq, d)
```

```python
import functools

import numpy as np
import jax
import jax.numpy as jnp
from jax import lax
from jax.experimental import pallas as pl
from jax.experimental.pallas import tpu as pltpu

D_MODEL = 1024
SEQ = 2048
DEPTH = 2
HEAD_DIM = 64
DIL_CONFIGS = ((128, 1), (512, 4), (2048, 16))
DILATIONS = tuple(d for _, d in DIL_CONFIGS)
N_DIL_GROUPS = 3
DIL_HEADS = 4
DIL_WIDTH = DIL_HEADS * HEAD_DIM
DIL_BLK = 128
DIL_UNITS = SEQ // DIL_BLK
MOBA_HEADS = 8
MOBA_WIDTH = MOBA_HEADS * HEAD_DIM
MOBA_BLOCK = 256
MOBA_NB = SEQ // MOBA_BLOCK
MOBA_TOPK = 3
RMS_EPS = 1e-6
NEG_INF = -1e30
QK_SCALE = HEAD_DIM ** -0.5

LANES = 128
A_QKV = 3 * DIL_WIDTH
B_QKV = 3 * MOBA_WIDTH
QKV_COLS = N_DIL_GROUPS * A_QKV + B_QKV
ZR_COLS = DIL_WIDTH + MOBA_WIDTH + 2 * D_MODEL
ROW_TILE = 512
VMEM_LIMIT = 56 * 1024 * 1024

F32 = jnp.float32
BF16 = jnp.bfloat16


def _slopes(n):
    return [float(np.float32(2.0 ** (-8.0 * (i + 1) / n))) for i in range(n)]


DIL_SLOPES = _slopes(N_DIL_GROUPS * DIL_HEADS)
MOBA_SLOPES = _slopes(MOBA_HEADS)


def _nt(a, b):
    return lax.dot_general(a, b, (((1,), (1,)), ((), ())), preferred_element_type=F32)


def _nn(a, b):
    return jnp.dot(a, b, preferred_element_type=F32)


def _bnt(a, b):
    return lax.dot_general(a, b, (((2,), (2,)), ((0,), (0,))), preferred_element_type=F32)


def _bnn(a, b):
    return lax.dot_general(a, b, (((2,), (1,)), ((0,), (0,))), preferred_element_type=F32)


def _rms(x, g):
    ms = jnp.mean(x * x, axis=-1, keepdims=True)
    return x * lax.rsqrt(ms + RMS_EPS) * g


def _head_masks():
    lane = lax.broadcasted_iota(jnp.int32, (1, LANES), 1)
    return lane < HEAD_DIM, lane >= HEAD_DIM


def _head_masks_bf16():
    m0, m1 = _head_masks()
    return jnp.where(m0, 1.0, 0.0).astype(BF16), jnp.where(m1, 1.0, 0.0).astype(BF16)


def _inproj_kernel(x_ref, g_ref, w_ref, a1_ref, a2_ref, a3_ref, b_ref, h_ref):
    h = _rms(x_ref[...], g_ref[...])
    hb = h.astype(BF16)
    a1_ref[...] = _nn(hb, w_ref[:, 0:A_QKV]).astype(BF16)
    for c0 in range(0, B_QKV, A_QKV):
        b_ref[:, c0:c0 + A_QKV] = _nn(hb, w_ref[:, 3 * A_QKV + c0:3 * A_QKV + c0 + A_QKV]).astype(BF16)
    nchunk = D_MODEL // LANES
    for j in range(nchunk):
        h_ref[j] = h[:, j * LANES:(j + 1) * LANES]
    for gi, o_ref in ((1, a2_ref), (2, a3_ref)):
        d = DILATIONS[gi]
        n = ROW_TILE // d
        hp = jnp.concatenate(
            [jnp.concatenate([h_ref[j, pl.ds(c, n, stride=d), :] for j in range(nchunk)], axis=1)
             for c in range(d)], axis=0).astype(BF16)
        r = _nn(hp, w_ref[:, gi * A_QKV:(gi + 1) * A_QKV]).astype(BF16)
        for c in range(d):
            o_ref[c] = r[c * n:(c + 1) * n]


def _inproj(x2d, g, w_qkv, batch):
    m = x2d.shape[0]
    tm = ROW_TILE
    per_b = SEQ // tm
    out_shape = [jax.ShapeDtypeStruct((m, A_QKV), BF16)]
    out_specs = [pl.BlockSpec((tm, A_QKV), lambda i: (i, 0))]
    for d in DILATIONS[1:]:
        out_shape.append(jax.ShapeDtypeStruct((batch * d, SEQ // d, A_QKV), BF16))
        out_specs.append(pl.BlockSpec((d, tm // d, A_QKV), lambda i: (i // per_b, i % per_b, 0)))
    out_shape.append(jax.ShapeDtypeStruct((m, B_QKV), BF16))
    out_specs.append(pl.BlockSpec((tm, B_QKV), lambda i: (i, 0)))
    return pl.pallas_call(
        _inproj_kernel,
        grid=(m // tm,),
        in_specs=[pl.BlockSpec((tm, D_MODEL), lambda i: (i, 0)),
                  pl.BlockSpec((1, D_MODEL), lambda i: (0, 0)),
                  pl.BlockSpec((D_MODEL, QKV_COLS), lambda i: (0, 0))],
        out_specs=out_specs,
        out_shape=out_shape,
        scratch_shapes=[pltpu.VMEM((D_MODEL // LANES, tm, LANES), F32)],
        compiler_params=pltpu.CompilerParams(dimension_semantics=("arbitrary",), vmem_limit_bytes=VMEM_LIMIT),
        name="inproj",
    )(x2d, g, w_qkv)


def _band_attend(q, kk, vv, bias_ref, g, pair, with_prev):
    m0, _ = _head_masks()
    hms = _head_masks_bf16()
    scale = jnp.asarray(QK_SCALE, BF16)
    pvs, ms = [], []
    for h in range(2):
        idx = g * DIL_HEADS + pair * 2 + h
        bias = bias_ref[idx] if with_prev else bias_ref[idx, :, DIL_BLK:]
        s = _bnt(q * (hms[h] * scale), kk) + bias
        m = jnp.max(s, axis=-1, keepdims=True)
        pvs.append(_bnn(jnp.exp(s - m).astype(BF16), vv * hms[h] + hms[1 - h]))
        ms.append(m)
    num = jnp.where(m0, pvs[0], pvs[1])
    den = pltpu.roll(jnp.where(m0, pvs[1], pvs[0]), HEAD_DIM, 2)
    return num / den, jnp.where(m0, ms[0], ms[1]) + jnp.log(den)


def _dilated_kernel(a1_ref, a2_ref, a3_ref, o1_ref, o2_ref, o3_ref, l1_ref, l2_ref, l3_ref, bias_ref):
    @pl.when(pl.program_id(0) == 0)
    def _():
        qi = lax.broadcasted_iota(jnp.int32, (DIL_BLK, 2 * DIL_BLK), 0)
        kj = lax.broadcasted_iota(jnp.int32, (DIL_BLK, 2 * DIL_BLK), 1)
        rel = DIL_BLK + qi - kj
        valid = (rel >= 0) & (rel <= DIL_BLK)
        for g, d in enumerate(DILATIONS):
            for h in range(DIL_HEADS):
                idx = g * DIL_HEADS + h
                bias_ref[idx] = jnp.where(valid, -DIL_SLOPES[idx] * (rel * d).astype(F32), NEG_INF)

    blk = DIL_BLK
    groups = ((a1_ref, o1_ref, l1_ref, 1), (a2_ref, o2_ref, l2_ref, 4), (a3_ref, o3_ref, l3_ref, 16))
    for g, (a_ref, o_ref, l_ref, nres) in enumerate(groups):
        nb = DIL_UNITS // nres
        for pair in range(2):
            lq = slice(pair * LANES, (pair + 1) * LANES)
            lk = slice(DIL_WIDTH + pair * LANES, DIL_WIDTH + (pair + 1) * LANES)
            lv = slice(2 * DIL_WIDTH + pair * LANES, 2 * DIL_WIDTH + (pair + 1) * LANES)

            def blocks(lanes):
                return a_ref[:, :, lanes].reshape(nres, nb, blk, LANES)

            q, k, v = blocks(lq), blocks(lk), blocks(lv)
            o_first, l_first = _band_attend(q[:, 0], k[:, 0], v[:, 0], bias_ref, g, pair, False)
            o_parts, l_parts = [o_first[:, None]], [l_first[:, None]]
            if nb > 1:
                e = nres * (nb - 1)
                kk = jnp.concatenate([k[:, :-1], k[:, 1:]], axis=2).reshape(e, 2 * blk, LANES)
                vv = jnp.concatenate([v[:, :-1], v[:, 1:]], axis=2).reshape(e, 2 * blk, LANES)
                o_rest, l_rest = _band_attend(q[:, 1:].reshape(e, blk, LANES), kk, vv, bias_ref, g, pair, True)
                o_parts.append(o_rest.reshape(nres, nb - 1, blk, LANES))
                l_parts.append(l_rest.reshape(nres, nb - 1, blk, LANES))
            o = jnp.concatenate(o_parts, axis=1) if nb > 1 else o_parts[0]
            l = jnp.concatenate(l_parts, axis=1) if nb > 1 else l_parts[0]
            o_ref[:, :, lq] = o.reshape(nres, nb * blk, LANES).astype(BF16)
            l_ref[:, :, lq] = l.reshape(nres, nb * blk, LANES)


def _dilated(a1, a2, a3, batch):
    ins = [a1.reshape(batch, SEQ, A_QKV), a2, a3]
    in_specs = [pl.BlockSpec((d, SEQ // d, A_QKV), lambda b: (b, 0, 0)) for d in DILATIONS]
    out_specs = [pl.BlockSpec((d, SEQ // d, DIL_WIDTH), lambda b: (b, 0, 0)) for d in DILATIONS] * 2
    out_shape = ([jax.ShapeDtypeStruct((batch * d, SEQ // d, DIL_WIDTH), BF16) for d in DILATIONS]
                 + [jax.ShapeDtypeStruct((batch * d, SEQ // d, DIL_WIDTH), F32) for d in DILATIONS])
    return pl.pallas_call(
        _dilated_kernel,
        grid=(batch,),
        in_specs=in_specs,
        out_specs=out_specs,
        out_shape=out_shape,
        scratch_shapes=[pltpu.VMEM((N_DIL_GROUPS * DIL_HEADS, DIL_BLK, 2 * DIL_BLK), F32)],
        compiler_params=pltpu.CompilerParams(dimension_semantics=("arbitrary",), vmem_limit_bytes=VMEM_LIMIT),
        name="dilated",
    )(*ins)


def _moba_kernel(q_ref, k_ref, v_ref, o_ref, kaug_ref, vaug_ref, qaug_ref, gtab_ref, causal_ref):
    b = pl.program_id(0)
    p = pl.program_id(1)
    blk = MOBA_BLOCK

    @pl.when((b == 0) & (p == 0))
    def _():
        u = lax.broadcasted_iota(jnp.int32, (1, SEQ), 1) - (SEQ - blk)
        for h in range(MOBA_HEADS):
            gtab_ref[h:h + 1, :] = MOBA_SLOPES[h] * u.astype(F32)
        qi = lax.broadcasted_iota(jnp.int32, (blk, blk), 0)
        kj = lax.broadcasted_iota(jnp.int32, (blk, blk), 1)
        causal_ref[...] = jnp.where(qi >= kj, 0.0, NEG_INF)
        row_blk = lax.broadcasted_iota(jnp.int32, (SEQ, LANES), 0) // blk
        lane_id = lax.broadcasted_iota(jnp.int32, (SEQ, LANES), 1)
        kaug_ref[:, LANES:] = jnp.where(row_blk == lane_id, 1.0, 0.0).astype(BF16)

    m0, m1 = _head_masks()
    hms = _head_masks_bf16()
    scale = jnp.asarray(QK_SCALE, BF16)
    eye = jnp.where(lax.broadcasted_iota(jnp.int32, (blk, blk), 0)
                    == lax.broadcasted_iota(jnp.int32, (blk, blk), 1), 1.0, 0.0).astype(BF16)
    k_all = k_ref[0]
    kaug_ref[:, :LANES] = k_all
    v_all = v_ref[0]
    for h in range(2):
        vaug_ref[h] = v_all * hms[h] + hms[1 - h]

    km = jnp.concatenate([jnp.mean(k_all[j * blk:(j + 1) * blk].astype(F32), axis=0, keepdims=True)
                          for j in range(MOBA_NB)], axis=0)

    q_all = q_ref[0]
    ridx = lax.broadcasted_iota(jnp.int32, (MOBA_NB, SEQ), 0)
    qblk = lax.broadcasted_iota(jnp.int32, (MOBA_NB, SEQ), 1) // blk
    past = ridx < qblk
    placed = []
    for h, hm in enumerate((m0, m1)):
        km_h = jnp.where(hm, km, 0.0)
        km_hi = km_h.astype(BF16)
        km_lo = (km_h - km_hi.astype(F32)).astype(BF16)
        gate = jnp.where(past, _nt(km_hi, q_all) + _nt(km_lo, q_all), NEG_INF)
        cnt = jnp.zeros((MOBA_NB, SEQ), jnp.int32)
        for jp in range(MOBA_NB):
            beats = (gate[jp:jp + 1, :] > gate) | ((gate[jp:jp + 1, :] == gate) & (jp < ridx))
            cnt = cnt + beats.astype(jnp.int32)
        sel = ((cnt < MOBA_TOPK) & past) | (ridx == qblk)
        selb = jnp.where(sel, 0.0, NEG_INF)
        placed.append(jnp.concatenate([selb, jnp.zeros((LANES - MOBA_NB, SEQ), F32)], axis=0).astype(BF16))

    def build(i):
        rows = slice(i * blk, (i + 1) * blk)
        for h in range(2):
            qaug_ref[i, h * blk:(h + 1) * blk, :LANES] = q_all[rows] * (hms[h] * scale)
            qaug_ref[i, h * blk:(h + 1) * blk, LANES:] = _nt(eye, placed[h][:, rows]).astype(BF16)

    def scores(i):
        nk = (i + 1) * blk
        s = _nt(qaug_ref[i], kaug_ref[0:nk, :])
        parts = []
        for h in range(2):
            sh = s[h * blk:(h + 1) * blk] + gtab_ref[pl.ds(2 * p + h, 1), (MOBA_NB - 1 - i) * blk:]
            s_own = sh[:, i * blk:] + causal_ref[...]
            m = jnp.max(s_own, axis=-1, keepdims=True)
            s_past = None
            if i > 0:
                s_past = sh[:, :i * blk]
                m = jnp.maximum(m, jnp.max(s_past, axis=-1, keepdims=True))
            parts.append((s_own, s_past, m))
        return parts

    def attend(i, parts):
        rows = slice(i * blk, (i + 1) * blk)
        outs = []
        for h, (s_own, s_past, m) in enumerate(parts):
            pv = _nn(jnp.exp(s_own - m).astype(BF16), vaug_ref[h, rows, :])
            if i > 0:
                pv = pv + _nn(jnp.exp(s_past - m).astype(BF16), vaug_ref[h, 0:i * blk, :])
            outs.append(pv / pltpu.roll(pv, HEAD_DIM, 1))
        o_ref[0, rows, :] = jnp.where(m0, outs[0], outs[1]).astype(BF16)

    order = list(range(MOBA_NB))
    build(order[0])
    parts = scores(order[0])
    for cur, nxt in zip(order[:-1], order[1:]):
        build(nxt)
        nxt_parts = scores(nxt)
        attend(cur, parts)
        parts = nxt_parts
    attend(order[-1], parts)


def _moba(qkv_b, batch):
    v3 = qkv_b.reshape(batch, SEQ, B_QKV)
    npairs = MOBA_HEADS // 2
    out = pl.pallas_call(
        _moba_kernel,
        grid=(batch, npairs),
        in_specs=[pl.BlockSpec((1, SEQ, LANES), lambda b, p: (b, 0, p)),
                  pl.BlockSpec((1, SEQ, LANES), lambda b, p: (b, 0, npairs + p)),
                  pl.BlockSpec((1, SEQ, LANES), lambda b, p: (b, 0, 2 * npairs + p))],
        out_specs=pl.BlockSpec((1, SEQ, LANES), lambda b, p: (b, 0, p)),
        out_shape=jax.ShapeDtypeStruct((batch, SEQ, MOBA_WIDTH), BF16),
        scratch_shapes=[pltpu.VMEM((SEQ, 2 * LANES), BF16),
                        pltpu.VMEM((2, SEQ, LANES), BF16),
                        pltpu.VMEM((MOBA_NB, 2 * MOBA_BLOCK, 2 * LANES), BF16),
                        pltpu.VMEM((MOBA_HEADS, SEQ), F32),
                        pltpu.VMEM((MOBA_BLOCK, MOBA_BLOCK), F32)],
        compiler_params=pltpu.CompilerParams(dimension_semantics=("arbitrary", "arbitrary"),
                                             vmem_limit_bytes=VMEM_LIMIT),
        name="moba",
    )(v3, v3, v3)
    return out.reshape(batch * SEQ, MOBA_WIDTH)


def _sigmoid(x):
    return 1.0 / (1.0 + jnp.exp(-x))


def _post_kernel(x_ref, o1_ref, o2_ref, o3_ref, l1_ref, l2_ref, l3_ref, ab_ref, g_ref, wzr_ref,
                 bm_ref, wpa_ref, wpb_ref, wout_ref, fg_ref, out_ref, o2s, o3s, l2s, l3s, *, final):
    x = x_ref[...]
    h = _rms(x, g_ref[...]).astype(BF16)

    def position_order(src, dst):
        d, n = src.shape[0], src.shape[1]
        for c in range(d):
            blk = src[c].astype(F32)
            for j in range(DIL_WIDTH // LANES):
                dst[j, pl.ds(c, n, stride=d), :] = blk[:, j * LANES:(j + 1) * LANES]
        return jnp.concatenate([dst[j] for j in range(DIL_WIDTH // LANES)], axis=1)

    cw = DIL_WIDTH
    zb0, ra0, rb0 = DIL_WIDTH, DIL_WIDTH + MOBA_WIDTH, DIL_WIDTH + MOBA_WIDTH + D_MODEL
    nchunk = D_MODEL // cw

    def gate_chunk(w0, b0, c):
        r = _nn(h, wzr_ref[:, w0 + c * cw:w0 + (c + 1) * cw]) + bm_ref[:, b0 + c * cw:b0 + (c + 1) * cw]
        return _sigmoid(r)

    z_a = _nn(h, wzr_ref[:, 0:zb0])
    gb = []
    for c in range(MOBA_WIDTH // cw):
        z_b = _nn(h, wzr_ref[:, zb0 + c * cw:zb0 + (c + 1) * cw])
        gb.append((ab_ref[:, c * cw:(c + 1) * cw].astype(F32) * (z_b * _sigmoid(z_b))).astype(BF16))
    gb = jnp.concatenate(gb, axis=1)

    o2, l2 = position_order(o2_ref, o2s), position_order(l2_ref, l2s)
    o3, l3 = position_order(o3_ref, o3s), position_order(l3_ref, l3s)
    l1 = l1_ref[...]
    lmax = jnp.maximum(jnp.maximum(l1, l2), l3)
    w1, w2, w3 = jnp.exp(l1 - lmax), jnp.exp(l2 - lmax), jnp.exp(l3 - lmax)
    attn_a = (w1 * o1_ref[...].astype(F32) + w2 * o2 + w3 * o3) / (w1 + w2 + w3)
    ga = (attn_a * (z_a * _sigmoid(z_a))).astype(BF16)

    gate_a = [gate_chunk(ra0, 0, c) for c in range(nchunk)]
    gate_b = [gate_chunk(rb0, D_MODEL, c) for c in range(nchunk)]
    merged = []
    for c in range(nchunk):
        cols = slice(c * cw, (c + 1) * cw)
        merged.append((gate_a[c] * _nn(ga, wpa_ref[:, cols]) + gate_b[c] * _nn(gb, wpb_ref[:, cols])).astype(BF16))
    xn = x + _nn(jnp.concatenate(merged, axis=1), wout_ref[...])
    if final:
        xn = _rms(xn, fg_ref[...])
    out_ref[...] = xn


def _post(x2d, o, l, ab, g, w_zr, bm, w_pa, w_pb, w_out, fg, final):
    m = x2d.shape[0]
    tm = ROW_TILE
    per_b = SEQ // tm
    row = lambda w: pl.BlockSpec((tm, w), lambda i: (i, 0))
    res = lambda d: pl.BlockSpec((d, tm // d, DIL_WIDTH), lambda i: (i // per_b, i % per_b, 0))
    full = lambda a: pl.BlockSpec(a.shape, lambda i: (0, 0), pipeline_mode=pl.Buffered(1))
    grp = [row(DIL_WIDTH), res(4), res(16)]
    return pl.pallas_call(
        functools.partial(_post_kernel, final=final),
        grid=(m // tm,),
        in_specs=[row(D_MODEL)] + grp + grp + [row(MOBA_WIDTH), full(g), full(w_zr), full(bm),
                                                full(w_pa), full(w_pb), full(w_out), full(fg)],
        out_specs=row(D_MODEL),
        out_shape=jax.ShapeDtypeStruct((m, D_MODEL), F32),
        scratch_shapes=[pltpu.VMEM((DIL_WIDTH // LANES, tm, LANES), F32)] * 4,
        compiler_params=pltpu.CompilerParams(dimension_semantics=("arbitrary",), vmem_limit_bytes=VMEM_LIMIT),
        name="post",
    )(x2d, *o, *l, ab, g, w_zr, bm, w_pa, w_pb, w_out, fg)


@jax.jit
def kernel(x, norm_g, w_in, b_merge, w_pa, w_pb, w_out, final_g):
    batch, seq, d = x.shape
    assert (seq, d) == (SEQ, D_MODEL)
    x2d = x.reshape(batch * seq, d)
    a_cols = N_DIL_GROUPS * A_QKV
    zb0 = a_cols + DIL_WIDTH
    zb1 = zb0 + B_QKV
    for layer in range(DEPTH):
        w = w_in[layer]
        w_qkv = jnp.concatenate([w[:, :a_cols], w[:, zb0:zb1]], axis=1).astype(BF16)
        w_zr = jnp.concatenate([w[:, a_cols:zb0], w[:, zb1:]], axis=1).astype(BF16)
        a1, a2, a3, qkv_b = _inproj(x2d, norm_g[layer][None, :], w_qkv, batch)
        outs = _dilated(a1, a2, a3, batch)
        ab = _moba(qkv_b, batch)
        o1 = outs[0].reshape(batch * seq, DIL_WIDTH)
        l1 = outs[3].reshape(batch * seq, DIL_WIDTH)
        x2d = _post(x2d, [o1, outs[1], outs[2]], [l1, outs[4], outs[5]], ab, norm_g[layer][None, :], w_zr,
                    b_merge[layer][None, :], w_pa[layer].astype(BF16), w_pb[layer].astype(BF16),
                    w_out[layer].astype(BF16), final_g[None, :], final=(layer == DEPTH - 1))
    return x2d.reshape(batch, seq, d)
```

```python
import functools

import numpy as np
import jax
import jax.numpy as jnp
from jax import lax
from jax.experimental import pallas as pl
from jax.experimental.pallas import tpu as pltpu

D_MODEL = 1024
SEQ = 2048
DEPTH = 2
HEAD_DIM = 64
DIL_CONFIGS = ((128, 1), (512, 4), (2048, 16))
DILATIONS = tuple(d for _, d in DIL_CONFIGS)
N_DIL_GROUPS = 3
DIL_HEADS = 4
DIL_WIDTH = DIL_HEADS * HEAD_DIM
DIL_BLK = 128
DIL_UNITS = SEQ // DIL_BLK
MOBA_HEADS = 8
MOBA_WIDTH = MOBA_HEADS * HEAD_DIM
MOBA_BLOCK = 256
MOBA_NB = SEQ // MOBA_BLOCK
MOBA_TOPK = 3
RMS_EPS = 1e-6
NEG_INF = -1e30
QK_SCALE = HEAD_DIM ** -0.5

LANES = 128
A_QKV = 3 * DIL_WIDTH
B_QKV = 3 * MOBA_WIDTH
COL_ZA = N_DIL_GROUPS * A_QKV
COL_BQKV = COL_ZA + DIL_WIDTH
COL_ZB = COL_BQKV + B_QKV
COL_RA = COL_ZB + MOBA_WIDTH
COL_RB = COL_RA + D_MODEL
IN_COLS = COL_RB + D_MODEL
ROW_TILE = 512
VMEM_LIMIT = 56 * 1024 * 1024

F32 = jnp.float32
BF16 = jnp.bfloat16


def _slopes(n):
    return [float(np.float32(2.0 ** (-8.0 * (i + 1) / n))) for i in range(n)]


DIL_SLOPES = _slopes(N_DIL_GROUPS * DIL_HEADS)
MOBA_SLOPES = _slopes(MOBA_HEADS)
assert all(np.frexp(s)[0] == 0.5 for s in MOBA_SLOPES)


def _nt(a, b):
    return lax.dot_general(a, b, (((1,), (1,)), ((), ())), preferred_element_type=F32)


def _nn(a, b):
    return jnp.dot(a, b, preferred_element_type=F32)


def _bnt(a, b):
    return lax.dot_general(a, b, (((2,), (2,)), ((0,), (0,))), preferred_element_type=F32)


def _bnn(a, b):
    return lax.dot_general(a, b, (((2,), (1,)), ((0,), (0,))), preferred_element_type=F32)


def _rms(x, g):
    ms = jnp.mean(x * x, axis=-1, keepdims=True)
    return x * lax.rsqrt(ms + RMS_EPS) * g


def _head_masks():
    lane = lax.broadcasted_iota(jnp.int32, (1, LANES), 1)
    return lane < HEAD_DIM, lane >= HEAD_DIM


def _head_masks_bf16():
    m0, m1 = _head_masks()
    return jnp.where(m0, 1.0, 0.0).astype(BF16), jnp.where(m1, 1.0, 0.0).astype(BF16)


def _inproj_kernel(x_ref, g_ref, w_ref, a1_ref, a2_ref, a3_ref, b_ref, h_ref):
    h = _rms(x_ref[...], g_ref[...])
    hb = h.astype(BF16)
    a1_ref[...] = _nn(hb, w_ref[:, 0:A_QKV]).astype(BF16)
    for c0 in range(0, B_QKV, A_QKV):
        b_ref[:, c0:c0 + A_QKV] = _nn(hb, w_ref[:, COL_BQKV + c0:COL_BQKV + c0 + A_QKV]).astype(BF16)
    nchunk = D_MODEL // LANES
    for j in range(nchunk):
        h_ref[j] = h[:, j * LANES:(j + 1) * LANES]
    for gi, o_ref in ((1, a2_ref), (2, a3_ref)):
        d = DILATIONS[gi]
        n = ROW_TILE // d
        hp = jnp.concatenate(
            [jnp.concatenate([h_ref[j, pl.ds(c, n, stride=d), :] for j in range(nchunk)], axis=1)
             for c in range(d)], axis=0).astype(BF16)
        r = _nn(hp, w_ref[:, gi * A_QKV:(gi + 1) * A_QKV]).astype(BF16)
        for c in range(d):
            o_ref[c] = r[c * n:(c + 1) * n]


def _inproj(x2d, g, w_qkv, batch):
    m = x2d.shape[0]
    tm = ROW_TILE
    per_b = SEQ // tm
    out_shape = [jax.ShapeDtypeStruct((m, A_QKV), BF16)]
    out_specs = [pl.BlockSpec((tm, A_QKV), lambda i: (i, 0))]
    for d in DILATIONS[1:]:
        out_shape.append(jax.ShapeDtypeStruct((batch * d, SEQ // d, A_QKV), BF16))
        out_specs.append(pl.BlockSpec((d, tm // d, A_QKV), lambda i: (i // per_b, i % per_b, 0)))
    out_shape.append(jax.ShapeDtypeStruct((m, B_QKV), BF16))
    out_specs.append(pl.BlockSpec((tm, B_QKV), lambda i: (i, 0)))
    return pl.pallas_call(
        _inproj_kernel,
        grid=(m // tm,),
        in_specs=[pl.BlockSpec((tm, D_MODEL), lambda i: (i, 0)),
                  pl.BlockSpec((1, D_MODEL), lambda i: (0, 0)),
                  pl.BlockSpec((D_MODEL, IN_COLS), lambda i: (0, 0), pipeline_mode=pl.Buffered(1))],
        out_specs=out_specs,
        out_shape=out_shape,
        scratch_shapes=[pltpu.VMEM((D_MODEL // LANES, tm, LANES), F32)],
        compiler_params=pltpu.CompilerParams(dimension_semantics=("arbitrary",), vmem_limit_bytes=VMEM_LIMIT),
        name="inproj",
    )(x2d, g, w_qkv)


def _band_attend(q, kk, vv, bias_ref, g, pair, with_prev):
    m0, _ = _head_masks()
    hms = _head_masks_bf16()
    scale = jnp.asarray(QK_SCALE, BF16)
    pvs, ms = [], []
    for h in range(2):
        idx = g * DIL_HEADS + pair * 2 + h
        bias = bias_ref[idx] if with_prev else bias_ref[idx, :, DIL_BLK:]
        s = _bnt(q * (hms[h] * scale), kk) + bias
        m = jnp.max(s, axis=-1, keepdims=True)
        pvs.append(_bnn(jnp.exp(s - m).astype(BF16), vv * hms[h] + hms[1 - h]))
        ms.append(m)
    num = jnp.where(m0, pvs[0], pvs[1])
    den = pltpu.roll(jnp.where(m0, pvs[1], pvs[0]), HEAD_DIM, 2)
    return num / den, jnp.where(m0, ms[0], ms[1]) + jnp.log(den)


def _dilated_kernel(a1_ref, a2_ref, a3_ref, o1_ref, o2_ref, o3_ref, l1_ref, l2_ref, l3_ref, bias_ref):
    @pl.when(pl.program_id(0) == 0)
    def _():
        qi = lax.broadcasted_iota(jnp.int32, (DIL_BLK, 2 * DIL_BLK), 0)
        kj = lax.broadcasted_iota(jnp.int32, (DIL_BLK, 2 * DIL_BLK), 1)
        rel = DIL_BLK + qi - kj
        valid = (rel >= 0) & (rel <= DIL_BLK)
        for g, d in enumerate(DILATIONS):
            for h in range(DIL_HEADS):
                idx = g * DIL_HEADS + h
                bias_ref[idx] = jnp.where(valid, -DIL_SLOPES[idx] * (rel * d).astype(F32), NEG_INF)

    blk = DIL_BLK
    groups = ((a1_ref, o1_ref, l1_ref, 1), (a2_ref, o2_ref, l2_ref, 4), (a3_ref, o3_ref, l3_ref, 16))
    for g, (a_ref, o_ref, l_ref, nres) in enumerate(groups):
        nb = DIL_UNITS // nres
        for pair in range(2):
            lq = slice(pair * LANES, (pair + 1) * LANES)
            lk = slice(DIL_WIDTH + pair * LANES, DIL_WIDTH + (pair + 1) * LANES)
            lv = slice(2 * DIL_WIDTH + pair * LANES, 2 * DIL_WIDTH + (pair + 1) * LANES)

            def blocks(lanes):
                return a_ref[:, :, lanes].reshape(nres, nb, blk, LANES)

            q, k, v = blocks(lq), blocks(lk), blocks(lv)
            o_first, l_first = _band_attend(q[:, 0], k[:, 0], v[:, 0], bias_ref, g, pair, False)
            o_parts, l_parts = [o_first[:, None]], [l_first[:, None]]
            if nb > 1:
                e = nres * (nb - 1)
                kk = jnp.concatenate([k[:, :-1], k[:, 1:]], axis=2).reshape(e, 2 * blk, LANES)
                vv = jnp.concatenate([v[:, :-1], v[:, 1:]], axis=2).reshape(e, 2 * blk, LANES)
                o_rest, l_rest = _band_attend(q[:, 1:].reshape(e, blk, LANES), kk, vv, bias_ref, g, pair, True)
                o_parts.append(o_rest.reshape(nres, nb - 1, blk, LANES))
                l_parts.append(l_rest.reshape(nres, nb - 1, blk, LANES))
            o = jnp.concatenate(o_parts, axis=1) if nb > 1 else o_parts[0]
            l = jnp.concatenate(l_parts, axis=1) if nb > 1 else l_parts[0]
            o_ref[:, :, lq] = o.reshape(nres, nb * blk, LANES).astype(BF16)
            l_ref[:, :, lq] = l.reshape(nres, nb * blk, LANES)


def _dilated(a1, a2, a3, batch):
    ins = [a1.reshape(batch, SEQ, A_QKV), a2, a3]
    in_specs = [pl.BlockSpec((d, SEQ // d, A_QKV), lambda b: (b, 0, 0)) for d in DILATIONS]
    out_specs = [pl.BlockSpec((d, SEQ // d, DIL_WIDTH), lambda b: (b, 0, 0)) for d in DILATIONS] * 2
    out_shape = ([jax.ShapeDtypeStruct((batch * d, SEQ // d, DIL_WIDTH), BF16) for d in DILATIONS]
                 + [jax.ShapeDtypeStruct((batch * d, SEQ // d, DIL_WIDTH), F32) for d in DILATIONS])
    return pl.pallas_call(
        _dilated_kernel,
        grid=(batch,),
        in_specs=in_specs,
        out_specs=out_specs,
        out_shape=out_shape,
        scratch_shapes=[pltpu.VMEM((N_DIL_GROUPS * DIL_HEADS, DIL_BLK, 2 * DIL_BLK), F32)],
        compiler_params=pltpu.CompilerParams(dimension_semantics=("arbitrary",), vmem_limit_bytes=VMEM_LIMIT),
        name="dilated",
    )(*ins)


def _moba_kernel(q_ref, k_ref, v_ref, o_ref, kaug_ref, vaug_ref, qaug_ref, slope_ref, causal_ref):
    b = pl.program_id(0)
    p = pl.program_id(1)
    blk = MOBA_BLOCK

    @pl.when((b == 0) & (p == 0))
    def _():
        for h in range(MOBA_HEADS):
            slope_ref[h:h + 1, :] = jnp.full((1, SEQ), MOBA_SLOPES[h], F32)
        qi = lax.broadcasted_iota(jnp.int32, (blk, blk), 0)
        kj = lax.broadcasted_iota(jnp.int32, (blk, blk), 1)
        causal_ref[...] = jnp.where(qi >= kj, 0.0, NEG_INF)
        row_id = lax.broadcasted_iota(jnp.int32, (SEQ, LANES), 0)
        lane_id = lax.broadcasted_iota(jnp.int32, (SEQ, LANES), 1)
        onehot = jnp.where(row_id // blk == lane_id, 1.0, 0.0)
        offset = (row_id % blk).astype(F32)
        kaug_ref[:, LANES:] = jnp.where(lane_id == MOBA_NB, offset, onehot).astype(BF16)

    m0, m1 = _head_masks()
    hms = _head_masks_bf16()
    scale = jnp.asarray(QK_SCALE, BF16)
    eye = jnp.where(lax.broadcasted_iota(jnp.int32, (blk, blk), 0)
                    == lax.broadcasted_iota(jnp.int32, (blk, blk), 1), 1.0, 0.0).astype(BF16)
    k_all = k_ref[0]
    kaug_ref[:, :LANES] = k_all
    v_all = v_ref[0]
    for h in range(2):
        vaug_ref[h] = v_all * hms[h] + hms[1 - h]

    km = jnp.concatenate([jnp.mean(k_all[j * blk:(j + 1) * blk].astype(F32), axis=0, keepdims=True)
                          for j in range(MOBA_NB)], axis=0)

    q_all = q_ref[0]
    ridx = lax.broadcasted_iota(jnp.int32, (MOBA_NB, SEQ), 0)
    qblk = lax.broadcasted_iota(jnp.int32, (MOBA_NB, SEQ), 1) // blk
    past = ridx < qblk
    km_terms = []
    for hm in (m0, m1):
        km_h = jnp.where(hm, km, 0.0)
        km_hi = km_h.astype(BF16).astype(F32)
        km_terms += [km_hi, km_h - km_hi]
    gates = _nt(jnp.concatenate(km_terms, axis=0).astype(BF16), q_all)
    for h in range(2):
        g0 = 2 * h * MOBA_NB
        gate = jnp.where(past, gates[g0:g0 + MOBA_NB] + gates[g0 + MOBA_NB:g0 + 2 * MOBA_NB], NEG_INF)
        cnt = jnp.zeros((MOBA_NB, SEQ), jnp.int32)
        for jp in range(MOBA_NB):
            beats = (gate[jp:jp + 1, :] > gate) | ((gate[jp:jp + 1, :] == gate) & (jp < ridx))
            cnt = cnt + beats.astype(jnp.int32)
        sel = ((cnt < MOBA_TOPK) & past) | (ridx == qblk)
        slope = slope_ref[pl.ds(2 * p + h, 1), :]
        selb = jnp.where(sel, slope * ((ridx - qblk) * blk).astype(F32), NEG_INF)
        slope_row = jnp.where(ridx == 0, slope, 0.0)
        placed = jnp.concatenate([selb, slope_row, jnp.zeros((LANES - 2 * MOBA_NB, SEQ), F32)],
                                 axis=0).astype(BF16)
        q_h = q_all * (hms[h] * scale)
        for i in range(MOBA_NB):
            rows = slice(i * blk, (i + 1) * blk)
            qaug_ref[i, h * blk:(h + 1) * blk, :LANES] = q_h[rows]
            qaug_ref[i, h * blk:(h + 1) * blk, LANES:] = _nt(eye, placed[:, rows]).astype(BF16)

    def scores(i):
        nk = (i + 1) * blk
        s = _nt(qaug_ref[i], kaug_ref[0:nk, :])
        parts = []
        for h in range(2):
            sh = s[h * blk:(h + 1) * blk]
            s_own = sh[:, i * blk:] + causal_ref[...]
            m = jnp.max(s_own, axis=-1, keepdims=True)
            s_past = None
            if i > 0:
                s_past = sh[:, :i * blk]
                m = jnp.maximum(m, jnp.max(s_past, axis=-1, keepdims=True))
            parts.append((s_own, s_past, m))
        return parts

    def attend(i, parts):
        rows = slice(i * blk, (i + 1) * blk)
        outs = []
        for h, (s_own, s_past, m) in enumerate(parts):
            pv = _nn(jnp.exp(s_own - m).astype(BF16), vaug_ref[h, rows, :])
            if i > 0:
                pv = pv + _nn(jnp.exp(s_past - m).astype(BF16), vaug_ref[h, 0:i * blk, :])
            outs.append(pv / pltpu.roll(pv, HEAD_DIM, 1))
        o_ref[0, rows, :] = jnp.where(m0, outs[0], outs[1]).astype(BF16)

    parts = scores(0)
    for i in range(1, MOBA_NB):
        nxt = scores(i)
        attend(i - 1, parts)
        parts = nxt
    attend(MOBA_NB - 1, parts)


def _moba(qkv_b, batch):
    v3 = qkv_b.reshape(batch, SEQ, B_QKV)
    npairs = MOBA_HEADS // 2
    out = pl.pallas_call(
        _moba_kernel,
        grid=(batch, npairs),
        in_specs=[pl.BlockSpec((1, SEQ, LANES), lambda b, p: (b, 0, p)),
                  pl.BlockSpec((1, SEQ, LANES), lambda b, p: (b, 0, npairs + p)),
                  pl.BlockSpec((1, SEQ, LANES), lambda b, p: (b, 0, 2 * npairs + p))],
        out_specs=pl.BlockSpec((1, SEQ, LANES), lambda b, p: (b, 0, p)),
        out_shape=jax.ShapeDtypeStruct((batch, SEQ, MOBA_WIDTH), BF16),
        scratch_shapes=[pltpu.VMEM((SEQ, 2 * LANES), BF16),
                        pltpu.VMEM((2, SEQ, LANES), BF16),
                        pltpu.VMEM((MOBA_NB, 2 * MOBA_BLOCK, 2 * LANES), BF16),
                        pltpu.VMEM((MOBA_HEADS, SEQ), F32),
                        pltpu.VMEM((MOBA_BLOCK, MOBA_BLOCK), F32)],
        compiler_params=pltpu.CompilerParams(dimension_semantics=("arbitrary", "arbitrary"),
                                             vmem_limit_bytes=VMEM_LIMIT),
        name="moba",
    )(v3, v3, v3)
    return out.reshape(batch * SEQ, MOBA_WIDTH)


def _sigmoid(x):
    return 1.0 / (1.0 + jnp.exp(-x))


def _post_kernel(x_ref, o1_ref, o2_ref, o3_ref, l1_ref, l2_ref, l3_ref, ab_ref, g_ref, wzr_ref,
                 bm_ref, wpa_ref, wpb_ref, wout_ref, fg_ref, out_ref, o2s, o3s, l2s, l3s, *, final):
    x = x_ref[...]
    h = _rms(x, g_ref[...]).astype(BF16)

    def position_order(src, dst):
        d, n = src.shape[0], src.shape[1]
        for c in range(d):
            blk = src[c].astype(F32)
            for j in range(DIL_WIDTH // LANES):
                dst[j, pl.ds(c, n, stride=d), :] = blk[:, j * LANES:(j + 1) * LANES]
        return jnp.concatenate([dst[j] for j in range(DIL_WIDTH // LANES)], axis=1)

    cw = DIL_WIDTH
    nchunk = D_MODEL // cw

    def gate_chunk(w0, b0, c):
        r = _nn(h, wzr_ref[:, w0 + c * cw:w0 + (c + 1) * cw]) + bm_ref[:, b0 + c * cw:b0 + (c + 1) * cw]
        return _sigmoid(r)

    z_a = _nn(h, wzr_ref[:, COL_ZA:COL_ZA + DIL_WIDTH])
    gb = []
    for c in range(MOBA_WIDTH // cw):
        z_b = _nn(h, wzr_ref[:, COL_ZB + c * cw:COL_ZB + (c + 1) * cw])
        gb.append((ab_ref[:, c * cw:(c + 1) * cw].astype(F32) * (z_b * _sigmoid(z_b))).astype(BF16))
    gb = jnp.concatenate(gb, axis=1)

    o2, l2 = position_order(o2_ref, o2s), position_order(l2_ref, l2s)
    o3, l3 = position_order(o3_ref, o3s), position_order(l3_ref, l3s)
    l1 = l1_ref[...]
    lmax = jnp.maximum(jnp.maximum(l1, l2), l3)
    w1, w2, w3 = jnp.exp(l1 - lmax), jnp.exp(l2 - lmax), jnp.exp(l3 - lmax)
    attn_a = (w1 * o1_ref[...].astype(F32) + w2 * o2 + w3 * o3) / (w1 + w2 + w3)
    ga = (attn_a * (z_a * _sigmoid(z_a))).astype(BF16)

    gate_a = [gate_chunk(COL_RA, 0, c) for c in range(nchunk)]
    gate_b = [gate_chunk(COL_RB, D_MODEL, c) for c in range(nchunk)]
    merged = []
    for c in range(nchunk):
        cols = slice(c * cw, (c + 1) * cw)
        merged.append((gate_a[c] * _nn(ga, wpa_ref[:, cols]) + gate_b[c] * _nn(gb, wpb_ref[:, cols])).astype(BF16))
    xn = x + _nn(jnp.concatenate(merged, axis=1), wout_ref[...])
    if final:
        xn = _rms(xn, fg_ref[...])
    out_ref[...] = xn


def _post(x2d, o, l, ab, g, w_zr, bm, w_pa, w_pb, w_out, fg, final):
    m = x2d.shape[0]
    tm = ROW_TILE
    per_b = SEQ // tm
    row = lambda w: pl.BlockSpec((tm, w), lambda i: (i, 0))
    res = lambda d: pl.BlockSpec((d, tm // d, DIL_WIDTH), lambda i: (i // per_b, i % per_b, 0))
    full = lambda a: pl.BlockSpec(a.shape, lambda i: (0, 0), pipeline_mode=pl.Buffered(1))
    grp = [row(DIL_WIDTH), res(4), res(16)]
    return pl.pallas_call(
        functools.partial(_post_kernel, final=final),
        grid=(m // tm,),
        in_specs=[row(D_MODEL)] + grp + grp + [row(MOBA_WIDTH), full(g), full(w_zr), full(bm),
                                                full(w_pa), full(w_pb), full(w_out), full(fg)],
        out_specs=row(D_MODEL),
        out_shape=jax.ShapeDtypeStruct((m, D_MODEL), F32),
        scratch_shapes=[pltpu.VMEM((DIL_WIDTH // LANES, tm, LANES), F32)] * 4,
        compiler_params=pltpu.CompilerParams(dimension_semantics=("arbitrary",), vmem_limit_bytes=VMEM_LIMIT),
        name="post",
    )(x2d, *o, *l, ab, g, w_zr, bm, w_pa, w_pb, w_out, fg)


@jax.jit
def kernel(x, norm_g, w_in, b_merge, w_pa, w_pb, w_out, final_g):
    batch, seq, d = x.shape
    assert (seq, d) == (SEQ, D_MODEL)
    x2d = x.reshape(batch * seq, d)
    w_in, w_pa, w_pb, w_out = (w.astype(BF16) for w in (w_in, w_pa, w_pb, w_out))
    for layer in range(DEPTH):
        a1, a2, a3, qkv_b = _inproj(x2d, norm_g[layer][None, :], w_in[layer], batch)
        outs = _dilated(a1, a2, a3, batch)
        ab = _moba(qkv_b, batch)
        o1 = outs[0].reshape(batch * seq, DIL_WIDTH)
        l1 = outs[3].reshape(batch * seq, DIL_WIDTH)
        x2d = _post(x2d, [o1, outs[1], outs[2]], [l1, outs[4], outs[5]], ab, norm_g[layer][None, :],
                    w_in[layer], b_merge[layer][None, :], w_pa[layer], w_pb[layer], w_out[layer],
                    final_g[None, :], final=(layer == DEPTH - 1))
    return x2d.reshape(batch, seq, d)
```

```python
import functools

import numpy as np
import jax
import jax.numpy as jnp
from jax import lax
from jax.experimental import pallas as pl
from jax.experimental.pallas import tpu as pltpu

D_MODEL = 1024
SEQ = 2048
DEPTH = 2
HEAD_DIM = 64
DIL_CONFIGS = ((128, 1), (512, 4), (2048, 16))
DILATIONS = tuple(d for _, d in DIL_CONFIGS)
N_DIL_GROUPS = 3
DIL_HEADS = 4
DIL_WIDTH = DIL_HEADS * HEAD_DIM
DIL_BLK = 128
DIL_UNITS = SEQ // DIL_BLK
MOBA_HEADS = 8
MOBA_WIDTH = MOBA_HEADS * HEAD_DIM
MOBA_BLOCK = 256
MOBA_NB = SEQ // MOBA_BLOCK
MOBA_TOPK = 3
RMS_EPS = 1e-6
NEG_INF = -1e30
QK_SCALE = HEAD_DIM ** -0.5

LANES = 128
A_QKV = 3 * DIL_WIDTH
B_QKV = 3 * MOBA_WIDTH
COL_ZA = N_DIL_GROUPS * A_QKV
COL_BQKV = COL_ZA + DIL_WIDTH
COL_ZB = COL_BQKV + B_QKV
COL_RA = COL_ZB + MOBA_WIDTH
COL_RB = COL_RA + D_MODEL
IN_COLS = COL_RB + D_MODEL
ROW_TILE = 512
VMEM_LIMIT = 56 * 1024 * 1024

F32 = jnp.float32
BF16 = jnp.bfloat16


def _slopes(n):
    return [float(np.float32(2.0 ** (-8.0 * (i + 1) / n))) for i in range(n)]


DIL_SLOPES = _slopes(N_DIL_GROUPS * DIL_HEADS)
MOBA_SLOPES = _slopes(MOBA_HEADS)
assert all(np.frexp(s)[0] == 0.5 for s in MOBA_SLOPES)


def _nt(a, b):
    return lax.dot_general(a, b, (((1,), (1,)), ((), ())), preferred_element_type=F32)


def _nn(a, b):
    return jnp.dot(a, b, preferred_element_type=F32)


def _bnt(a, b):
    return lax.dot_general(a, b, (((2,), (2,)), ((0,), (0,))), preferred_element_type=F32)


def _bnn(a, b):
    return lax.dot_general(a, b, (((2,), (1,)), ((0,), (0,))), preferred_element_type=F32)


def _rms(x, g):
    ms = jnp.mean(x * x, axis=-1, keepdims=True)
    return x * lax.rsqrt(ms + RMS_EPS) * g


def _head_masks():
    lane = lax.broadcasted_iota(jnp.int32, (1, LANES), 1)
    return lane < HEAD_DIM, lane >= HEAD_DIM


def _head_masks_bf16():
    m0, m1 = _head_masks()
    return jnp.where(m0, 1.0, 0.0).astype(BF16), jnp.where(m1, 1.0, 0.0).astype(BF16)


def _inproj_kernel(x_ref, g_ref, w_ref, a1_ref, a2_ref, a3_ref, b_ref, h_ref):
    h = _rms(x_ref[...], g_ref[0])
    hb = h.astype(BF16)
    a1_ref[...] = _nn(hb, w_ref[0, :, 0:A_QKV]).astype(BF16)
    for c0 in range(0, B_QKV, A_QKV):
        b_ref[:, c0:c0 + A_QKV] = _nn(hb, w_ref[0, :, COL_BQKV + c0:COL_BQKV + c0 + A_QKV]).astype(BF16)
    nchunk = D_MODEL // LANES
    for j in range(nchunk):
        h_ref[j] = h[:, j * LANES:(j + 1) * LANES]
    for gi, o_ref in ((1, a2_ref), (2, a3_ref)):
        d = DILATIONS[gi]
        n = ROW_TILE // d
        hp = jnp.concatenate(
            [jnp.concatenate([h_ref[j, pl.ds(c, n, stride=d), :] for j in range(nchunk)], axis=1)
             for c in range(d)], axis=0).astype(BF16)
        r = _nn(hp, w_ref[0, :, gi * A_QKV:(gi + 1) * A_QKV]).astype(BF16)
        for c in range(d):
            o_ref[c] = r[c * n:(c + 1) * n]


def _layer_spec(a, layer):
    return pl.BlockSpec((1,) + a.shape[1:], lambda i: (layer, 0, 0), pipeline_mode=pl.Buffered(1))


def _inproj(x2d, g, w_in, layer, batch):
    m = x2d.shape[0]
    tm = ROW_TILE
    per_b = SEQ // tm
    out_shape = [jax.ShapeDtypeStruct((m, A_QKV), BF16)]
    out_specs = [pl.BlockSpec((tm, A_QKV), lambda i: (i, 0))]
    for d in DILATIONS[1:]:
        out_shape.append(jax.ShapeDtypeStruct((batch * d, SEQ // d, A_QKV), BF16))
        out_specs.append(pl.BlockSpec((d, tm // d, A_QKV), lambda i: (i // per_b, i % per_b, 0)))
    out_shape.append(jax.ShapeDtypeStruct((m, B_QKV), BF16))
    out_specs.append(pl.BlockSpec((tm, B_QKV), lambda i: (i, 0)))
    return pl.pallas_call(
        _inproj_kernel,
        grid=(m // tm,),
        in_specs=[pl.BlockSpec((tm, D_MODEL), lambda i: (i, 0)), _layer_spec(g, layer), _layer_spec(w_in, layer)],
        out_specs=out_specs,
        out_shape=out_shape,
        scratch_shapes=[pltpu.VMEM((D_MODEL // LANES, tm, LANES), F32)],
        compiler_params=pltpu.CompilerParams(dimension_semantics=("arbitrary",), vmem_limit_bytes=VMEM_LIMIT),
        name="inproj",
    )(x2d, g, w_in)


def _band_attend(q, kk, vv, bias_ref, g, pair, with_prev):
    m0, _ = _head_masks()
    hms = _head_masks_bf16()
    scale = jnp.asarray(QK_SCALE, BF16)
    pvs, ms = [], []
    vv1 = jnp.concatenate([vv, jnp.ones_like(vv)], axis=2)
    for h in range(2):
        idx = g * DIL_HEADS + pair * 2 + h
        bias = bias_ref[idx] if with_prev else bias_ref[idx, :, DIL_BLK:]
        s = _bnt(q * (hms[h] * scale), kk) + bias
        m = jnp.max(s, axis=-1, keepdims=True)
        pvs.append(_bnn(jnp.exp(s - m).astype(BF16), vv1))
        ms.append(m)
    num = jnp.where(m0, pvs[0][:, :, :LANES], pvs[1][:, :, :LANES])
    den = jnp.where(m0, pvs[0][:, :, LANES:], pvs[1][:, :, LANES:])
    return num / den, jnp.where(m0, ms[0], ms[1]) + jnp.log(den)


def _dilated_kernel(a1_ref, a2_ref, a3_ref, o1_ref, o2_ref, o3_ref, l1_ref, l2_ref, l3_ref, bias_ref):
    @pl.when(pl.program_id(0) == 0)
    def _():
        qi = lax.broadcasted_iota(jnp.int32, (DIL_BLK, 2 * DIL_BLK), 0)
        kj = lax.broadcasted_iota(jnp.int32, (DIL_BLK, 2 * DIL_BLK), 1)
        rel = DIL_BLK + qi - kj
        valid = (rel >= 0) & (rel <= DIL_BLK)
        for g, d in enumerate(DILATIONS):
            for h in range(DIL_HEADS):
                idx = g * DIL_HEADS + h
                bias_ref[idx] = jnp.where(valid, -DIL_SLOPES[idx] * (rel * d).astype(F32), NEG_INF)

    blk = DIL_BLK
    groups = ((a1_ref, o1_ref, l1_ref, 1), (a2_ref, o2_ref, l2_ref, 4), (a3_ref, o3_ref, l3_ref, 16))
    for g, (a_ref, o_ref, l_ref, nres) in enumerate(groups):
        nb = DIL_UNITS // nres
        for pair in range(2):
            lq = slice(pair * LANES, (pair + 1) * LANES)
            lk = slice(DIL_WIDTH + pair * LANES, DIL_WIDTH + (pair + 1) * LANES)
            lv = slice(2 * DIL_WIDTH + pair * LANES, 2 * DIL_WIDTH + (pair + 1) * LANES)

            def blocks(lanes):
                return a_ref[:, :, lanes].reshape(nres, nb, blk, LANES)

            q, k, v = blocks(lq), blocks(lk), blocks(lv)
            o_first, l_first = _band_attend(q[:, 0], k[:, 0], v[:, 0], bias_ref, g, pair, False)
            o_parts, l_parts = [o_first[:, None]], [l_first[:, None]]
            if nb > 1:
                e = nres * (nb - 1)
                kk = jnp.concatenate([k[:, :-1], k[:, 1:]], axis=2).reshape(e, 2 * blk, LANES)
                vv = jnp.concatenate([v[:, :-1], v[:, 1:]], axis=2).reshape(e, 2 * blk, LANES)
                o_rest, l_rest = _band_attend(q[:, 1:].reshape(e, blk, LANES), kk, vv, bias_ref, g, pair, True)
                o_parts.append(o_rest.reshape(nres, nb - 1, blk, LANES))
                l_parts.append(l_rest.reshape(nres, nb - 1, blk, LANES))
            o = jnp.concatenate(o_parts, axis=1) if nb > 1 else o_parts[0]
            l = jnp.concatenate(l_parts, axis=1) if nb > 1 else l_parts[0]
            o_ref[:, :, lq] = o.reshape(nres, nb * blk, LANES).astype(BF16)
            l_ref[:, :, lq] = l.reshape(nres, nb * blk, LANES)


def _dilated(a1, a2, a3, batch):
    ins = [a1.reshape(batch, SEQ, A_QKV), a2, a3]
    in_specs = [pl.BlockSpec((d, SEQ // d, A_QKV), lambda b: (b, 0, 0)) for d in DILATIONS]
    out_specs = [pl.BlockSpec((d, SEQ // d, DIL_WIDTH), lambda b: (b, 0, 0)) for d in DILATIONS] * 2
    out_shape = ([jax.ShapeDtypeStruct((batch * d, SEQ // d, DIL_WIDTH), BF16) for d in DILATIONS]
                 + [jax.ShapeDtypeStruct((batch * d, SEQ // d, DIL_WIDTH), F32) for d in DILATIONS])
    return pl.pallas_call(
        _dilated_kernel,
        grid=(batch,),
        in_specs=in_specs,
        out_specs=out_specs,
        out_shape=out_shape,
        scratch_shapes=[pltpu.VMEM((N_DIL_GROUPS * DIL_HEADS, DIL_BLK, 2 * DIL_BLK), F32)],
        compiler_params=pltpu.CompilerParams(dimension_semantics=("arbitrary",), vmem_limit_bytes=VMEM_LIMIT),
        name="dilated",
    )(*ins)


def _moba_kernel(q_ref, k_ref, v_ref, o_ref, kaug_ref, vaug_ref, qaug_ref, slope_ref, causal_ref):
    b = pl.program_id(0)
    p = pl.program_id(1)
    blk = MOBA_BLOCK

    @pl.when((b == 0) & (p == 0))
    def _():
        for h in range(MOBA_HEADS):
            slope_ref[h:h + 1, :] = jnp.full((1, SEQ), MOBA_SLOPES[h], F32)
        qi = lax.broadcasted_iota(jnp.int32, (blk, blk), 0)
        kj = lax.broadcasted_iota(jnp.int32, (blk, blk), 1)
        causal_ref[...] = jnp.where(qi >= kj, 0.0, NEG_INF)
        row_id = lax.broadcasted_iota(jnp.int32, (SEQ, LANES), 0)
        lane_id = lax.broadcasted_iota(jnp.int32, (SEQ, LANES), 1)
        onehot = jnp.where(row_id // blk == lane_id, 1.0, 0.0)
        offset = (row_id % blk).astype(F32)
        kaug_ref[:, LANES:] = jnp.where(lane_id == MOBA_NB, offset, onehot).astype(BF16)
        vaug_ref[:, LANES:] = jnp.ones((SEQ, LANES), BF16)

    m0, m1 = _head_masks()
    hms = _head_masks_bf16()
    scale = jnp.asarray(QK_SCALE, BF16)
    eye = jnp.where(lax.broadcasted_iota(jnp.int32, (blk, blk), 0)
                    == lax.broadcasted_iota(jnp.int32, (blk, blk), 1), 1.0, 0.0).astype(BF16)
    k_all = k_ref[0]
    kaug_ref[:, :LANES] = k_all
    vaug_ref[:, :LANES] = v_ref[0]

    km = jnp.concatenate([jnp.mean(k_all[j * blk:(j + 1) * blk].astype(F32), axis=0, keepdims=True)
                          for j in range(MOBA_NB)], axis=0)

    q_all = q_ref[0]
    ridx = lax.broadcasted_iota(jnp.int32, (MOBA_NB, SEQ), 0)
    qblk = lax.broadcasted_iota(jnp.int32, (MOBA_NB, SEQ), 1) // blk
    past = ridx < qblk
    km_terms = []
    for hm in (m0, m1):
        km_h = jnp.where(hm, km, 0.0)
        km_hi = km_h.astype(BF16).astype(F32)
        km_terms += [km_hi, km_h - km_hi]
    gates = _nt(jnp.concatenate(km_terms, axis=0).astype(BF16), q_all)
    for h in range(2):
        g0 = 2 * h * MOBA_NB
        gate = jnp.where(past, gates[g0:g0 + MOBA_NB] + gates[g0 + MOBA_NB:g0 + 2 * MOBA_NB], NEG_INF)
        cnt = jnp.zeros((MOBA_NB, SEQ), jnp.int32)
        for jp in range(MOBA_NB):
            beats = (gate[jp:jp + 1, :] > gate) | ((gate[jp:jp + 1, :] == gate) & (jp < ridx))
            cnt = cnt + beats.astype(jnp.int32)
        sel = ((cnt < MOBA_TOPK) & past) | (ridx == qblk)
        slope = slope_ref[pl.ds(2 * p + h, 1), :]
        selb = jnp.where(sel, slope * ((ridx - qblk) * blk).astype(F32), NEG_INF)
        slope_row = jnp.where(ridx == 0, slope, 0.0)
        placed = jnp.concatenate([selb, slope_row, jnp.zeros((LANES - 2 * MOBA_NB, SEQ), F32)],
                                 axis=0).astype(BF16)
        q_h = q_all * (hms[h] * scale)
        for i in range(MOBA_NB):
            rows = slice(i * blk, (i + 1) * blk)
            qaug_ref[i, h * blk:(h + 1) * blk, :LANES] = q_h[rows]
            qaug_ref[i, h * blk:(h + 1) * blk, LANES:] = _nt(eye, placed[:, rows]).astype(BF16)

    def scores(i):
        nk = (i + 1) * blk
        s = _nt(qaug_ref[i], kaug_ref[0:nk, :])
        parts = []
        for h in range(2):
            sh = s[h * blk:(h + 1) * blk]
            s_own = sh[:, i * blk:] + causal_ref[...]
            m = jnp.max(s_own, axis=-1, keepdims=True)
            s_past = None
            if i > 0:
                s_past = sh[:, :i * blk]
                m = jnp.maximum(m, jnp.max(s_past, axis=-1, keepdims=True))
            parts.append((s_own, s_past, m))
        return parts

    def attend(i, parts):
        rows = slice(i * blk, (i + 1) * blk)
        pvs = []
        for s_own, s_past, m in parts:
            pv = _nn(jnp.exp(s_own - m).astype(BF16), vaug_ref[rows, :])
            if i > 0:
                pv = pv + _nn(jnp.exp(s_past - m).astype(BF16), vaug_ref[0:i * blk, :])
            pvs.append(pv)
        num = jnp.where(m0, pvs[0][:, :LANES], pvs[1][:, :LANES])
        den = jnp.where(m0, pvs[0][:, LANES:], pvs[1][:, LANES:])
        o_ref[0, rows, :] = (num / den).astype(BF16)

    parts = scores(0)
    for i in range(1, MOBA_NB):
        nxt = scores(i)
        attend(i - 1, parts)
        parts = nxt
    attend(MOBA_NB - 1, parts)


def _moba(qkv_b, batch):
    v3 = qkv_b.reshape(batch, SEQ, B_QKV)
    npairs = MOBA_HEADS // 2
    out = pl.pallas_call(
        _moba_kernel,
        grid=(batch, npairs),
        in_specs=[pl.BlockSpec((1, SEQ, LANES), lambda b, p: (b, 0, p)),
                  pl.BlockSpec((1, SEQ, LANES), lambda b, p: (b, 0, npairs + p)),
                  pl.BlockSpec((1, SEQ, LANES), lambda b, p: (b, 0, 2 * npairs + p))],
        out_specs=pl.BlockSpec((1, SEQ, LANES), lambda b, p: (b, 0, p)),
        out_shape=jax.ShapeDtypeStruct((batch, SEQ, MOBA_WIDTH), BF16),
        scratch_shapes=[pltpu.VMEM((SEQ, 2 * LANES), BF16),
                        pltpu.VMEM((SEQ, 2 * LANES), BF16),
                        pltpu.VMEM((MOBA_NB, 2 * MOBA_BLOCK, 2 * LANES), BF16),
                        pltpu.VMEM((MOBA_HEADS, SEQ), F32),
                        pltpu.VMEM((MOBA_BLOCK, MOBA_BLOCK), F32)],
        compiler_params=pltpu.CompilerParams(dimension_semantics=("arbitrary", "arbitrary"),
                                             vmem_limit_bytes=VMEM_LIMIT),
        name="moba",
    )(v3, v3, v3)
    return out.reshape(batch * SEQ, MOBA_WIDTH)


def _sigmoid(x):
    return 1.0 / (1.0 + jnp.exp(-x))


def _post_kernel(x_ref, o1_ref, o2_ref, o3_ref, l1_ref, l2_ref, l3_ref, ab_ref, g_ref, wzr_ref,
                 bm_ref, wpa_ref, wpb_ref, wout_ref, fg_ref, out_ref, o2s, o3s, l2s, l3s, *, final):
    x = x_ref[...]
    h = _rms(x, g_ref[0]).astype(BF16)

    def position_order(src, dst):
        d, n = src.shape[0], src.shape[1]
        for c in range(d):
            blk = src[c].astype(F32)
            for j in range(DIL_WIDTH // LANES):
                dst[j, pl.ds(c, n, stride=d), :] = blk[:, j * LANES:(j + 1) * LANES]
        return jnp.concatenate([dst[j] for j in range(DIL_WIDTH // LANES)], axis=1)

    cw = DIL_WIDTH
    nchunk = D_MODEL // cw

    def gate_chunk(w0, b0, c):
        r = _nn(h, wzr_ref[0, :, w0 + c * cw:w0 + (c + 1) * cw]) + bm_ref[0, :, b0 + c * cw:b0 + (c + 1) * cw]
        return _sigmoid(r)

    z_a = _nn(h, wzr_ref[0, :, COL_ZA:COL_ZA + DIL_WIDTH])
    gb = []
    for c in range(MOBA_WIDTH // cw):
        z_b = _nn(h, wzr_ref[0, :, COL_ZB + c * cw:COL_ZB + (c + 1) * cw])
        gb.append((ab_ref[:, c * cw:(c + 1) * cw].astype(F32) * (z_b * _sigmoid(z_b))).astype(BF16))
    gb = jnp.concatenate(gb, axis=1)

    o2, l2 = position_order(o2_ref, o2s), position_order(l2_ref, l2s)
    o3, l3 = position_order(o3_ref, o3s), position_order(l3_ref, l3s)
    l1 = l1_ref[...]
    lmax = jnp.maximum(jnp.maximum(l1, l2), l3)
    w1, w2, w3 = jnp.exp(l1 - lmax), jnp.exp(l2 - lmax), jnp.exp(l3 - lmax)
    attn_a = (w1 * o1_ref[...].astype(F32) + w2 * o2 + w3 * o3) / (w1 + w2 + w3)
    ga = (attn_a * (z_a * _sigmoid(z_a))).astype(BF16)

    gate_a = [gate_chunk(COL_RA, 0, c) for c in range(nchunk)]
    gate_b = [gate_chunk(COL_RB, D_MODEL, c) for c in range(nchunk)]
    merged = []
    for c in range(nchunk):
        cols = slice(c * cw, (c + 1) * cw)
        merged.append((gate_a[c] * _nn(ga, wpa_ref[0, :, cols])
                       + gate_b[c] * _nn(gb, wpb_ref[0, :, cols])).astype(BF16))
    xn = x + _nn(jnp.concatenate(merged, axis=1), wout_ref[0])
    if final:
        xn = _rms(xn, fg_ref[0])
    out_ref[...] = xn


def _post(x2d, o, l, ab, g, w_in, bm, w_pa, w_pb, w_out, fg, layer):
    m = x2d.shape[0]
    tm = ROW_TILE
    per_b = SEQ // tm
    row = lambda w: pl.BlockSpec((tm, w), lambda i: (i, 0))
    res = lambda d: pl.BlockSpec((d, tm // d, DIL_WIDTH), lambda i: (i // per_b, i % per_b, 0))
    grp = [row(DIL_WIDTH), res(4), res(16)]
    params = (g, w_in, bm, w_pa, w_pb, w_out)
    return pl.pallas_call(
        functools.partial(_post_kernel, final=(layer == DEPTH - 1)),
        grid=(m // tm,),
        in_specs=([row(D_MODEL)] + grp + grp + [row(MOBA_WIDTH)] + [_layer_spec(a, layer) for a in params]
                  + [_layer_spec(fg, 0)]),
        out_specs=row(D_MODEL),
        out_shape=jax.ShapeDtypeStruct((m, D_MODEL), F32),
        scratch_shapes=[pltpu.VMEM((DIL_WIDTH // LANES, tm, LANES), F32)] * 4,
        compiler_params=pltpu.CompilerParams(dimension_semantics=("arbitrary",), vmem_limit_bytes=VMEM_LIMIT),
        name="post",
    )(x2d, *o, *l, ab, *params, fg)


@jax.jit
def kernel(x, norm_g, w_in, b_merge, w_pa, w_pb, w_out, final_g):
    batch, seq, d = x.shape
    assert (seq, d) == (SEQ, D_MODEL)
    x2d = x.reshape(batch * seq, d)
    w_in, w_pa, w_pb, w_out = (w.astype(BF16) for w in (w_in, w_pa, w_pb, w_out))
    norm_g, b_merge, final_g = norm_g[:, None, :], b_merge[:, None, :], final_g[None, None, :]
    for layer in range(DEPTH):
        a1, a2, a3, qkv_b = _inproj(x2d, norm_g, w_in, layer, batch)
        outs = _dilated(a1, a2, a3, batch)
        ab = _moba(qkv_b, batch)
        o1 = outs[0].reshape(batch * seq, DIL_WIDTH)
        l1 = outs[3].reshape(batch * seq, DIL_WIDTH)
        x2d = _post(x2d, [o1, outs[1], outs[2]], [l1, outs[4], outs[5]], ab, norm_g, w_in, b_merge,
                    w_pa, w_pb, w_out, final_g, layer)
    return x2d.reshape(batch, seq, d)
```

```python
import functools

import numpy as np
import jax
import jax.numpy as jnp
from jax import lax
from jax.experimental import pallas as pl
from jax.experimental.pallas import tpu as pltpu

D_MODEL = 1024
SEQ = 2048
DEPTH = 2
HEAD_DIM = 64
DIL_CONFIGS = ((128, 1), (512, 4), (2048, 16))
DILATIONS = tuple(d for _, d in DIL_CONFIGS)
N_DIL_GROUPS = 3
DIL_HEADS = 4
DIL_WIDTH = DIL_HEADS * HEAD_DIM
DIL_BLK = 128
DIL_UNITS = SEQ // DIL_BLK
MOBA_HEADS = 8
MOBA_WIDTH = MOBA_HEADS * HEAD_DIM
MOBA_BLOCK = 256
MOBA_NB = SEQ // MOBA_BLOCK
MOBA_TOPK = 3
MOBA_PAIRS_PER_STEP = 2
RMS_EPS = 1e-6
NEG_INF = -1e30
QK_SCALE = HEAD_DIM ** -0.5

LANES = 128
A_QKV = 3 * DIL_WIDTH
B_QKV = 3 * MOBA_WIDTH
COL_ZA = N_DIL_GROUPS * A_QKV
COL_BQKV = COL_ZA + DIL_WIDTH
COL_ZB = COL_BQKV + B_QKV
COL_RA = COL_ZB + MOBA_WIDTH
COL_RB = COL_RA + D_MODEL
IN_COLS = COL_RB + D_MODEL
ROW_TILE = 512
INPROJ_SUBTILES = 2
VMEM_LIMIT = 56 * 1024 * 1024

F32 = jnp.float32
BF16 = jnp.bfloat16


def _slopes(n):
    return [float(np.float32(2.0 ** (-8.0 * (i + 1) / n))) for i in range(n)]


DIL_SLOPES = _slopes(N_DIL_GROUPS * DIL_HEADS)
MOBA_SLOPES = _slopes(MOBA_HEADS)
assert all(np.frexp(s)[0] == 0.5 for s in MOBA_SLOPES)


def _nt(a, b):
    return lax.dot_general(a, b, (((1,), (1,)), ((), ())), preferred_element_type=F32)


def _nn(a, b):
    return jnp.dot(a, b, preferred_element_type=F32)


def _bnt(a, b):
    return lax.dot_general(a, b, (((2,), (2,)), ((0,), (0,))), preferred_element_type=F32)


def _bnn(a, b):
    return lax.dot_general(a, b, (((2,), (1,)), ((0,), (0,))), preferred_element_type=F32)


def _rms(x, g):
    ms = jnp.mean(x * x, axis=-1, keepdims=True)
    return x * lax.rsqrt(ms + RMS_EPS) * g


def _head_masks():
    lane = lax.broadcasted_iota(jnp.int32, (1, LANES), 1)
    return lane < HEAD_DIM, lane >= HEAD_DIM


def _head_masks_bf16():
    m0, m1 = _head_masks()
    return jnp.where(m0, 1.0, 0.0).astype(BF16), jnp.where(m1, 1.0, 0.0).astype(BF16)


def _inproj_kernel(x_ref, g_ref, w_ref, a1_ref, a2_ref, a3_ref, b_ref, h_ref):
    nchunk = D_MODEL // LANES
    for t in range(INPROJ_SUBTILES):
        rows = slice(t * ROW_TILE, (t + 1) * ROW_TILE)
        h = _rms(x_ref[rows, :], g_ref[0])
        hb = h.astype(BF16)
        a1_ref[rows, :] = _nn(hb, w_ref[0, :, 0:A_QKV]).astype(BF16)
        for c0 in range(0, B_QKV, A_QKV):
            b_ref[rows, c0:c0 + A_QKV] = _nn(
                hb, w_ref[0, :, COL_BQKV + c0:COL_BQKV + c0 + A_QKV]).astype(BF16)
        for j in range(nchunk):
            h_ref[t, j] = h[:, j * LANES:(j + 1) * LANES]
        for gi, o_ref in ((1, a2_ref), (2, a3_ref)):
            d = DILATIONS[gi]
            n = ROW_TILE // d
            hp = jnp.concatenate(
                [jnp.concatenate([h_ref[t, j, pl.ds(c, n, stride=d), :] for j in range(nchunk)], axis=1)
                 for c in range(d)], axis=0).astype(BF16)
            r = _nn(hp, w_ref[0, :, gi * A_QKV:(gi + 1) * A_QKV]).astype(BF16)
            for c in range(d):
                o_ref[c, t * n:(t + 1) * n, :] = r[c * n:(c + 1) * n]


def _layer_spec(a, layer):
    return pl.BlockSpec((1,) + a.shape[1:], lambda i: (layer, 0, 0), pipeline_mode=pl.Buffered(1))


def _inproj(x2d, g, w_in, layer, batch):
    m = x2d.shape[0]
    tm = INPROJ_SUBTILES * ROW_TILE
    per_b = SEQ // tm
    out_shape = [jax.ShapeDtypeStruct((m, A_QKV), BF16)]
    out_specs = [pl.BlockSpec((tm, A_QKV), lambda i: (i, 0))]
    for d in DILATIONS[1:]:
        out_shape.append(jax.ShapeDtypeStruct((batch * d, SEQ // d, A_QKV), BF16))
        out_specs.append(pl.BlockSpec((d, tm // d, A_QKV), lambda i: (i // per_b, i % per_b, 0)))
    out_shape.append(jax.ShapeDtypeStruct((m, B_QKV), BF16))
    out_specs.append(pl.BlockSpec((tm, B_QKV), lambda i: (i, 0)))
    return pl.pallas_call(
        _inproj_kernel,
        grid=(m // tm,),
        in_specs=[pl.BlockSpec((tm, D_MODEL), lambda i: (i, 0)), _layer_spec(g, layer), _layer_spec(w_in, layer)],
        out_specs=out_specs,
        out_shape=out_shape,
        scratch_shapes=[pltpu.VMEM((INPROJ_SUBTILES, D_MODEL // LANES, ROW_TILE, LANES), F32)],
        compiler_params=pltpu.CompilerParams(dimension_semantics=("arbitrary",), vmem_limit_bytes=VMEM_LIMIT),
        name="inproj",
    )(x2d, g, w_in)


def _band_attend(q, kk, vv, bias_ref, g, pair, with_prev):
    m0, _ = _head_masks()
    hms = _head_masks_bf16()
    scale = jnp.asarray(QK_SCALE, BF16)
    pvs, ms = [], []
    vv1 = jnp.concatenate([vv, jnp.ones_like(vv)], axis=2)
    for h in range(2):
        idx = g * DIL_HEADS + pair * 2 + h
        bias = bias_ref[idx] if with_prev else bias_ref[idx, :, DIL_BLK:]
        s = _bnt(q * (hms[h] * scale), kk) + bias
        m = jnp.max(s, axis=-1, keepdims=True)
        pvs.append(_bnn(jnp.exp(s - m).astype(BF16), vv1))
        ms.append(m)
    num = jnp.where(m0, pvs[0][:, :, :LANES], pvs[1][:, :, :LANES])
    den = jnp.where(m0, pvs[0][:, :, LANES:], pvs[1][:, :, LANES:])
    return num / den, jnp.where(m0, ms[0], ms[1]) + jnp.log(den)


def _dilated_kernel(a1_ref, a2_ref, a3_ref, o1_ref, o2_ref, o3_ref, l1_ref, l2_ref, l3_ref, bias_ref):
    @pl.when(pl.program_id(0) == 0)
    def _():
        qi = lax.broadcasted_iota(jnp.int32, (DIL_BLK, 2 * DIL_BLK), 0)
        kj = lax.broadcasted_iota(jnp.int32, (DIL_BLK, 2 * DIL_BLK), 1)
        rel = DIL_BLK + qi - kj
        valid = (rel >= 0) & (rel <= DIL_BLK)
        for g, d in enumerate(DILATIONS):
            for h in range(DIL_HEADS):
                idx = g * DIL_HEADS + h
                bias_ref[idx] = jnp.where(valid, -DIL_SLOPES[idx] * (rel * d).astype(F32), NEG_INF)

    blk = DIL_BLK
    groups = ((a1_ref, o1_ref, l1_ref, 1), (a2_ref, o2_ref, l2_ref, 4), (a3_ref, o3_ref, l3_ref, 16))
    for g, (a_ref, o_ref, l_ref, nres) in enumerate(groups):
        nb = DIL_UNITS // nres
        for pair in range(2):
            lq = slice(pair * LANES, (pair + 1) * LANES)
            lk = slice(DIL_WIDTH + pair * LANES, DIL_WIDTH + (pair + 1) * LANES)
            lv = slice(2 * DIL_WIDTH + pair * LANES, 2 * DIL_WIDTH + (pair + 1) * LANES)

            def blocks(lanes):
                return a_ref[:, :, lanes].reshape(nres, nb, blk, LANES)

            q, k, v = blocks(lq), blocks(lk), blocks(lv)
            o_first, l_first = _band_attend(q[:, 0], k[:, 0], v[:, 0], bias_ref, g, pair, False)
            o_parts, l_parts = [o_first[:, None]], [l_first[:, None]]
            if nb > 1:
                e = nres * (nb - 1)
                kk = jnp.concatenate([k[:, :-1], k[:, 1:]], axis=2).reshape(e, 2 * blk, LANES)
                vv = jnp.concatenate([v[:, :-1], v[:, 1:]], axis=2).reshape(e, 2 * blk, LANES)
                o_rest, l_rest = _band_attend(q[:, 1:].reshape(e, blk, LANES), kk, vv, bias_ref, g, pair, True)
                o_parts.append(o_rest.reshape(nres, nb - 1, blk, LANES))
                l_parts.append(l_rest.reshape(nres, nb - 1, blk, LANES))
            o = jnp.concatenate(o_parts, axis=1) if nb > 1 else o_parts[0]
            l = jnp.concatenate(l_parts, axis=1) if nb > 1 else l_parts[0]
            o_ref[:, :, lq] = o.reshape(nres, nb * blk, LANES).astype(BF16)
            l_ref[:, :, lq] = l.reshape(nres, nb * blk, LANES)


def _dilated(a1, a2, a3, batch):
    ins = [a1.reshape(batch, SEQ, A_QKV), a2, a3]
    in_specs = [pl.BlockSpec((d, SEQ // d, A_QKV), lambda b: (b, 0, 0)) for d in DILATIONS]
    out_specs = [pl.BlockSpec((d, SEQ // d, DIL_WIDTH), lambda b: (b, 0, 0)) for d in DILATIONS] * 2
    out_shape = ([jax.ShapeDtypeStruct((batch * d, SEQ // d, DIL_WIDTH), BF16) for d in DILATIONS]
                 + [jax.ShapeDtypeStruct((batch * d, SEQ // d, DIL_WIDTH), F32) for d in DILATIONS])
    return pl.pallas_call(
        _dilated_kernel,
        grid=(batch,),
        in_specs=in_specs,
        out_specs=out_specs,
        out_shape=out_shape,
        scratch_shapes=[pltpu.VMEM((N_DIL_GROUPS * DIL_HEADS, DIL_BLK, 2 * DIL_BLK), F32)],
        compiler_params=pltpu.CompilerParams(dimension_semantics=("arbitrary",), vmem_limit_bytes=VMEM_LIMIT),
        name="dilated",
    )(*ins)


def _moba_kernel(q_ref, k_ref, v_ref, o_ref, kaug_all, vaug_all, qaug_all, slope_ref, causal_ref):
    b = pl.program_id(0)
    p = pl.program_id(1)
    blk = MOBA_BLOCK

    @pl.when((b == 0) & (p == 0))
    def _():
        for h in range(MOBA_HEADS):
            slope_ref[h:h + 1, :] = jnp.full((1, SEQ), MOBA_SLOPES[h], F32)
        qi = lax.broadcasted_iota(jnp.int32, (blk, blk), 0)
        kj = lax.broadcasted_iota(jnp.int32, (blk, blk), 1)
        causal_ref[...] = jnp.where(qi >= kj, 0.0, NEG_INF)
        row_id = lax.broadcasted_iota(jnp.int32, (SEQ, LANES), 0)
        lane_id = lax.broadcasted_iota(jnp.int32, (SEQ, LANES), 1)
        onehot = jnp.where(row_id // blk == lane_id, 1.0, 0.0)
        offset = (row_id % blk).astype(F32)
        for sp in range(MOBA_PAIRS_PER_STEP):
            kaug_all[sp, :, LANES:] = jnp.where(lane_id == MOBA_NB, offset, onehot).astype(BF16)
            vaug_all[sp, :, LANES:] = jnp.ones((SEQ, LANES), BF16)

    m0, m1 = _head_masks()
    hms = _head_masks_bf16()
    scale = jnp.asarray(QK_SCALE, BF16)
    eye = jnp.where(lax.broadcasted_iota(jnp.int32, (blk, blk), 0)
                    == lax.broadcasted_iota(jnp.int32, (blk, blk), 1), 1.0, 0.0).astype(BF16)
    for sp in range(MOBA_PAIRS_PER_STEP):
        _moba_pair(sp, MOBA_PAIRS_PER_STEP * p + sp, q_ref, k_ref, v_ref, o_ref, kaug_all.at[sp],
                   vaug_all.at[sp], qaug_all.at[sp], slope_ref, causal_ref, m0, m1, hms, scale, eye)


def _moba_pair(sp, pair, q_ref, k_ref, v_ref, o_ref, kaug_ref, vaug_ref, qaug_ref, slope_ref, causal_ref,
               m0, m1, hms, scale, eye):
    blk = MOBA_BLOCK
    sl = slice(sp * LANES, (sp + 1) * LANES)
    k_all = k_ref[0, :, sl]
    kaug_ref[:, :LANES] = k_all
    vaug_ref[:, :LANES] = v_ref[0, :, sl]

    km = jnp.concatenate([jnp.mean(k_all[j * blk:(j + 1) * blk].astype(F32), axis=0, keepdims=True)
                          for j in range(MOBA_NB)], axis=0)

    q_all = q_ref[0, :, sl]
    ridx = lax.broadcasted_iota(jnp.int32, (MOBA_NB, SEQ), 0)
    qblk = lax.broadcasted_iota(jnp.int32, (MOBA_NB, SEQ), 1) // blk
    past = ridx < qblk
    km_terms = []
    for hm in (m0, m1):
        km_h = jnp.where(hm, km, 0.0)
        km_hi = km_h.astype(BF16).astype(F32)
        km_terms += [km_hi, km_h - km_hi]
    gates = _nt(jnp.concatenate(km_terms, axis=0).astype(BF16), q_all)
    for h in range(2):
        g0 = 2 * h * MOBA_NB
        gate = jnp.where(past, gates[g0:g0 + MOBA_NB] + gates[g0 + MOBA_NB:g0 + 2 * MOBA_NB], NEG_INF)
        cnt = jnp.zeros((MOBA_NB, SEQ), jnp.int32)
        for jp in range(MOBA_NB):
            beats = (gate[jp:jp + 1, :] > gate) | ((gate[jp:jp + 1, :] == gate) & (jp < ridx))
            cnt = cnt + beats.astype(jnp.int32)
        sel = ((cnt < MOBA_TOPK) & past) | (ridx == qblk)
        slope = slope_ref[pl.ds(2 * pair + h, 1), :]
        selb = jnp.where(sel, slope * ((ridx - qblk) * blk).astype(F32), NEG_INF)
        slope_row = jnp.where(ridx == 0, slope, 0.0)
        placed = jnp.concatenate([selb, slope_row, jnp.zeros((LANES - 2 * MOBA_NB, SEQ), F32)],
                                 axis=0).astype(BF16)
        q_h = q_all * (hms[h] * scale)
        for i in range(MOBA_NB):
            rows = slice(i * blk, (i + 1) * blk)
            qaug_ref[i, h * blk:(h + 1) * blk, :LANES] = q_h[rows]
            qaug_ref[i, h * blk:(h + 1) * blk, LANES:] = _nt(eye, placed[:, rows]).astype(BF16)

    def scores(i):
        nk = (i + 1) * blk
        s = _nt(qaug_ref[i], kaug_ref[0:nk, :])
        parts = []
        for h in range(2):
            sh = s[h * blk:(h + 1) * blk]
            s_own = sh[:, i * blk:] + causal_ref[...]
            m = jnp.max(s_own, axis=-1, keepdims=True)
            s_past = None
            if i > 0:
                s_past = sh[:, :i * blk]
                m = jnp.maximum(m, jnp.max(s_past, axis=-1, keepdims=True))
            parts.append((s_own, s_past, m))
        return parts

    def attend(i, parts):
        rows = slice(i * blk, (i + 1) * blk)
        pvs = []
        for s_own, s_past, m in parts:
            pv = _nn(jnp.exp(s_own - m).astype(BF16), vaug_ref[rows, :])
            if i > 0:
                pv = pv + _nn(jnp.exp(s_past - m).astype(BF16), vaug_ref[0:i * blk, :])
            pvs.append(pv)
        num = jnp.where(m0, pvs[0][:, :LANES], pvs[1][:, :LANES])
        den = jnp.where(m0, pvs[0][:, LANES:], pvs[1][:, LANES:])
        o_ref[0, rows, sl] = (num / den).astype(BF16)

    parts = scores(0)
    for i in range(1, MOBA_NB):
        nxt = scores(i)
        attend(i - 1, parts)
        parts = nxt
    attend(MOBA_NB - 1, parts)


def _moba(qkv_b, batch):
    v3 = qkv_b.reshape(batch, SEQ, B_QKV)
    pps = MOBA_PAIRS_PER_STEP
    nsteps = MOBA_HEADS // 2 // pps
    width = pps * LANES
    out = pl.pallas_call(
        _moba_kernel,
        grid=(batch, nsteps),
        in_specs=[pl.BlockSpec((1, SEQ, width), lambda b, p: (b, 0, p)),
                  pl.BlockSpec((1, SEQ, width), lambda b, p: (b, 0, nsteps + p)),
                  pl.BlockSpec((1, SEQ, width), lambda b, p: (b, 0, 2 * nsteps + p))],
        out_specs=pl.BlockSpec((1, SEQ, width), lambda b, p: (b, 0, p)),
        out_shape=jax.ShapeDtypeStruct((batch, SEQ, MOBA_WIDTH), BF16),
        scratch_shapes=[pltpu.VMEM((pps, SEQ, 2 * LANES), BF16),
                        pltpu.VMEM((pps, SEQ, 2 * LANES), BF16),
                        pltpu.VMEM((pps, MOBA_NB, 2 * MOBA_BLOCK, 2 * LANES), BF16),
                        pltpu.VMEM((MOBA_HEADS, SEQ), F32),
                        pltpu.VMEM((MOBA_BLOCK, MOBA_BLOCK), F32)],
        compiler_params=pltpu.CompilerParams(dimension_semantics=("arbitrary", "arbitrary"),
                                             vmem_limit_bytes=VMEM_LIMIT),
        name="moba",
    )(v3, v3, v3)
    return out.reshape(batch * SEQ, MOBA_WIDTH)


def _sigmoid(x):
    return 1.0 / (1.0 + jnp.exp(-x))


def _post_kernel(x_ref, o1_ref, o2_ref, o3_ref, l1_ref, l2_ref, l3_ref, ab_ref, g_ref, wzr_ref,
                 bm_ref, wpa_ref, wpb_ref, wout_ref, fg_ref, out_ref, o2s, o3s, l2s, l3s, *, final):
    x = x_ref[...]
    h = _rms(x, g_ref[0]).astype(BF16)

    def position_order(src, dst):
        d, n = src.shape[0], src.shape[1]
        for c in range(d):
            blk = src[c].astype(F32)
            for j in range(DIL_WIDTH // LANES):
                dst[j, pl.ds(c, n, stride=d), :] = blk[:, j * LANES:(j + 1) * LANES]
        return jnp.concatenate([dst[j] for j in range(DIL_WIDTH // LANES)], axis=1)

    cw = DIL_WIDTH
    nchunk = D_MODEL // cw

    def gate_chunk(w0, b0, c):
        r = _nn(h, wzr_ref[0, :, w0 + c * cw:w0 + (c + 1) * cw]) + bm_ref[0, :, b0 + c * cw:b0 + (c + 1) * cw]
        return _sigmoid(r)

    z_a = _nn(h, wzr_ref[0, :, COL_ZA:COL_ZA + DIL_WIDTH])
    gb = []
    for c in range(MOBA_WIDTH // cw):
        z_b = _nn(h, wzr_ref[0, :, COL_ZB + c * cw:COL_ZB + (c + 1) * cw])
        gb.append((ab_ref[:, c * cw:(c + 1) * cw].astype(F32) * (z_b * _sigmoid(z_b))).astype(BF16))
    gb = jnp.concatenate(gb, axis=1)

    o2, l2 = position_order(o2_ref, o2s), position_order(l2_ref, l2s)
    o3, l3 = position_order(o3_ref, o3s), position_order(l3_ref, l3s)
    l1 = l1_ref[...]
    lmax = jnp.maximum(jnp.maximum(l1, l2), l3)
    w1, w2, w3 = jnp.exp(l1 - lmax), jnp.exp(l2 - lmax), jnp.exp(l3 - lmax)
    attn_a = (w1 * o1_ref[...].astype(F32) + w2 * o2 + w3 * o3) / (w1 + w2 + w3)
    ga = (attn_a * (z_a * _sigmoid(z_a))).astype(BF16)

    gate_a = [gate_chunk(COL_RA, 0, c) for c in range(nchunk)]
    gate_b = [gate_chunk(COL_RB, D_MODEL, c) for c in range(nchunk)]
    merged = []
    for c in range(nchunk):
        cols = slice(c * cw, (c + 1) * cw)
        merged.append((gate_a[c] * _nn(ga, wpa_ref[0, :, cols])
                       + gate_b[c] * _nn(gb, wpb_ref[0, :, cols])).astype(BF16))
    xn = x + _nn(jnp.concatenate(merged, axis=1), wout_ref[0])
    if final:
        xn = _rms(xn, fg_ref[0])
    out_ref[...] = xn


def _post(x2d, o, l, ab, g, w_in, bm, w_pa, w_pb, w_out, fg, layer):
    m = x2d.shape[0]
    tm = ROW_TILE
    per_b = SEQ // tm
    row = lambda w: pl.BlockSpec((tm, w), lambda i: (i, 0))
    res = lambda d: pl.BlockSpec((d, tm // d, DIL_WIDTH), lambda i: (i // per_b, i % per_b, 0))
    grp = [row(DIL_WIDTH), res(4), res(16)]
    params = (g, w_in, bm, w_pa, w_pb, w_out)
    return pl.pallas_call(
        functools.partial(_post_kernel, final=(layer == DEPTH - 1)),
        grid=(m // tm,),
        in_specs=([row(D_MODEL)] + grp + grp + [row(MOBA_WIDTH)] + [_layer_spec(a, layer) for a in params]
                  + [_layer_spec(fg, 0)]),
        out_specs=row(D_MODEL),
        out_shape=jax.ShapeDtypeStruct((m, D_MODEL), F32),
        scratch_shapes=[pltpu.VMEM((DIL_WIDTH // LANES, tm, LANES), F32)] * 4,
        compiler_params=pltpu.CompilerParams(dimension_semantics=("arbitrary",), vmem_limit_bytes=VMEM_LIMIT),
        name="post",
    )(x2d, *o, *l, ab, *params, fg)


@jax.jit
def kernel(x, norm_g, w_in, b_merge, w_pa, w_pb, w_out, final_g):
    batch, seq, d = x.shape
    assert (seq, d) == (SEQ, D_MODEL)
    x2d = x.reshape(batch * seq, d)
    w_in, w_pa, w_pb, w_out = (w.astype(BF16) for w in (w_in, w_pa, w_pb, w_out))
    norm_g, b_merge, final_g = norm_g[:, None, :], b_merge[:, None, :], final_g[None, None, :]
    for layer in range(DEPTH):
        a1, a2, a3, qkv_b = _inproj(x2d, norm_g, w_in, layer, batch)
        outs = _dilated(a1, a2, a3, batch)
        ab = _moba(qkv_b, batch)
        o1 = outs[0].reshape(batch * seq, DIL_WIDTH)
        l1 = outs[3].reshape(batch * seq, DIL_WIDTH)
        x2d = _post(x2d, [o1, outs[1], outs[2]], [l1, outs[4], outs[5]], ab, norm_g, w_in, b_merge,
                    w_pa, w_pb, w_out, final_g, layer)
    return x2d.reshape(batch, seq, d)
```

```python
import functools

import numpy as np
import jax
import jax.numpy as jnp
from jax import lax
from jax.experimental import pallas as pl
from jax.experimental.pallas import tpu as pltpu

D_MODEL = 1024
SEQ = 2048
DEPTH = 2
HEAD_DIM = 64
DIL_CONFIGS = ((128, 1), (512, 4), (2048, 16))
DILATIONS = tuple(d for _, d in DIL_CONFIGS)
N_DIL_GROUPS = 3
DIL_HEADS = 4
DIL_WIDTH = DIL_HEADS * HEAD_DIM
DIL_BLK = 128
DIL_UNITS = SEQ // DIL_BLK
MOBA_HEADS = 8
MOBA_WIDTH = MOBA_HEADS * HEAD_DIM
MOBA_BLOCK = 256
MOBA_NB = SEQ // MOBA_BLOCK
MOBA_TOPK = 3
MOBA_PAIRS_PER_STEP = 2
RMS_EPS = 1e-6
NEG_INF = -1e30
QK_SCALE = HEAD_DIM ** -0.5

LANES = 128
A_QKV = 3 * DIL_WIDTH
B_QKV = 3 * MOBA_WIDTH
COL_ZA = N_DIL_GROUPS * A_QKV
COL_BQKV = COL_ZA + DIL_WIDTH
COL_ZB = COL_BQKV + B_QKV
COL_RA = COL_ZB + MOBA_WIDTH
COL_RB = COL_RA + D_MODEL
IN_COLS = COL_RB + D_MODEL
ROW_TILE = 512
INPROJ_SUBTILES = 2
POST_SUBTILES = 2
VMEM_LIMIT = 56 * 1024 * 1024

F32 = jnp.float32
BF16 = jnp.bfloat16


def _slopes(n):
    return [float(np.float32(2.0 ** (-8.0 * (i + 1) / n))) for i in range(n)]


DIL_SLOPES = _slopes(N_DIL_GROUPS * DIL_HEADS)
MOBA_SLOPES = _slopes(MOBA_HEADS)
assert all(np.frexp(s)[0] == 0.5 for s in MOBA_SLOPES)


def _nt(a, b):
    return lax.dot_general(a, b, (((1,), (1,)), ((), ())), preferred_element_type=F32)


def _nn(a, b):
    return jnp.dot(a, b, preferred_element_type=F32)


def _bnt(a, b):
    return lax.dot_general(a, b, (((2,), (2,)), ((0,), (0,))), preferred_element_type=F32)


def _bnn(a, b):
    return lax.dot_general(a, b, (((2,), (1,)), ((0,), (0,))), preferred_element_type=F32)


def _rms(x, g):
    ms = jnp.mean(x * x, axis=-1, keepdims=True)
    return x * lax.rsqrt(ms + RMS_EPS) * g


def _head_masks():
    lane = lax.broadcasted_iota(jnp.int32, (1, LANES), 1)
    return lane < HEAD_DIM, lane >= HEAD_DIM


def _head_masks_bf16():
    m0, m1 = _head_masks()
    return jnp.where(m0, 1.0, 0.0).astype(BF16), jnp.where(m1, 1.0, 0.0).astype(BF16)


def _inproj_kernel(x_ref, g_ref, w_ref, a1_ref, a2_ref, a3_ref, b_ref, h_ref):
    nchunk = D_MODEL // LANES
    for t in range(INPROJ_SUBTILES):
        rows = slice(t * ROW_TILE, (t + 1) * ROW_TILE)
        h = _rms(x_ref[rows, :], g_ref[0])
        hb = h.astype(BF16)
        a1_ref[rows, :] = _nn(hb, w_ref[0, :, 0:A_QKV]).astype(BF16)
        for c0 in range(0, B_QKV, A_QKV):
            b_ref[rows, c0:c0 + A_QKV] = _nn(
                hb, w_ref[0, :, COL_BQKV + c0:COL_BQKV + c0 + A_QKV]).astype(BF16)
        for j in range(nchunk):
            h_ref[t, j] = h[:, j * LANES:(j + 1) * LANES]
        for gi, o_ref in ((1, a2_ref), (2, a3_ref)):
            d = DILATIONS[gi]
            n = ROW_TILE // d
            hp = jnp.concatenate(
                [jnp.concatenate([h_ref[t, j, pl.ds(c, n, stride=d), :] for j in range(nchunk)], axis=1)
                 for c in range(d)], axis=0).astype(BF16)
            r = _nn(hp, w_ref[0, :, gi * A_QKV:(gi + 1) * A_QKV]).astype(BF16)
            for c in range(d):
                o_ref[c, t * n:(t + 1) * n, :] = r[c * n:(c + 1) * n]


def _layer_spec(a, layer):
    return pl.BlockSpec((1,) + a.shape[1:], lambda i: (layer, 0, 0), pipeline_mode=pl.Buffered(1))


def _inproj(x2d, g, w_in, layer, batch):
    m = x2d.shape[0]
    tm = INPROJ_SUBTILES * ROW_TILE
    per_b = SEQ // tm
    out_shape = [jax.ShapeDtypeStruct((m, A_QKV), BF16)]
    out_specs = [pl.BlockSpec((tm, A_QKV), lambda i: (i, 0))]
    for d in DILATIONS[1:]:
        out_shape.append(jax.ShapeDtypeStruct((batch * d, SEQ // d, A_QKV), BF16))
        out_specs.append(pl.BlockSpec((d, tm // d, A_QKV), lambda i: (i // per_b, i % per_b, 0)))
    out_shape.append(jax.ShapeDtypeStruct((m, B_QKV), BF16))
    out_specs.append(pl.BlockSpec((tm, B_QKV), lambda i: (i, 0)))
    return pl.pallas_call(
        _inproj_kernel,
        grid=(m // tm,),
        in_specs=[pl.BlockSpec((tm, D_MODEL), lambda i: (i, 0)), _layer_spec(g, layer), _layer_spec(w_in, layer)],
        out_specs=out_specs,
        out_shape=out_shape,
        scratch_shapes=[pltpu.VMEM((INPROJ_SUBTILES, D_MODEL // LANES, ROW_TILE, LANES), F32)],
        compiler_params=pltpu.CompilerParams(dimension_semantics=("arbitrary",), vmem_limit_bytes=VMEM_LIMIT),
        name="inproj",
    )(x2d, g, w_in)


def _band_attend(q, kk, vv, bias_ref, g, pair, with_prev):
    m0, _ = _head_masks()
    hms = _head_masks_bf16()
    scale = jnp.asarray(QK_SCALE, BF16)
    pvs, ms = [], []
    vv1 = jnp.concatenate([vv, jnp.ones_like(vv)], axis=2)
    for h in range(2):
        idx = g * DIL_HEADS + pair * 2 + h
        bias = bias_ref[idx] if with_prev else bias_ref[idx, :, DIL_BLK:]
        s = _bnt(q * (hms[h] * scale), kk) + bias
        m = jnp.max(s, axis=-1, keepdims=True)
        pvs.append(_bnn(jnp.exp(s - m).astype(BF16), vv1))
        ms.append(m)
    num = jnp.where(m0, pvs[0][:, :, :LANES], pvs[1][:, :, :LANES])
    den = jnp.where(m0, pvs[0][:, :, LANES:], pvs[1][:, :, LANES:])
    return num / den, jnp.where(m0, ms[0], ms[1]) + jnp.log(den)


def _dilated_kernel(a1_ref, a2_ref, a3_ref, o1_ref, o2_ref, o3_ref, l1_ref, l2_ref, l3_ref, bias_ref):
    @pl.when(pl.program_id(0) == 0)
    def _():
        qi = lax.broadcasted_iota(jnp.int32, (DIL_BLK, 2 * DIL_BLK), 0)
        kj = lax.broadcasted_iota(jnp.int32, (DIL_BLK, 2 * DIL_BLK), 1)
        rel = DIL_BLK + qi - kj
        valid = (rel >= 0) & (rel <= DIL_BLK)
        for g, d in enumerate(DILATIONS):
            for h in range(DIL_HEADS):
                idx = g * DIL_HEADS + h
                bias_ref[idx] = jnp.where(valid, -DIL_SLOPES[idx] * (rel * d).astype(F32), NEG_INF)

    blk = DIL_BLK
    groups = ((a1_ref, o1_ref, l1_ref, 1), (a2_ref, o2_ref, l2_ref, 4), (a3_ref, o3_ref, l3_ref, 16))
    for g, (a_ref, o_ref, l_ref, nres) in enumerate(groups):
        nb = DIL_UNITS // nres
        for pair in range(2):
            lq = slice(pair * LANES, (pair + 1) * LANES)
            lk = slice(DIL_WIDTH + pair * LANES, DIL_WIDTH + (pair + 1) * LANES)
            lv = slice(2 * DIL_WIDTH + pair * LANES, 2 * DIL_WIDTH + (pair + 1) * LANES)

            def blocks(lanes):
                return a_ref[:, :, lanes].reshape(nres, nb, blk, LANES)

            q, k, v = blocks(lq), blocks(lk), blocks(lv)
            o_first, l_first = _band_attend(q[:, 0], k[:, 0], v[:, 0], bias_ref, g, pair, False)
            o_parts, l_parts = [o_first[:, None]], [l_first[:, None]]
            if nb > 1:
                e = nres * (nb - 1)
                kk = jnp.concatenate([k[:, :-1], k[:, 1:]], axis=2).reshape(e, 2 * blk, LANES)
                vv = jnp.concatenate([v[:, :-1], v[:, 1:]], axis=2).reshape(e, 2 * blk, LANES)
                o_rest, l_rest = _band_attend(q[:, 1:].reshape(e, blk, LANES), kk, vv, bias_ref, g, pair, True)
                o_parts.append(o_rest.reshape(nres, nb - 1, blk, LANES))
                l_parts.append(l_rest.reshape(nres, nb - 1, blk, LANES))
            o = jnp.concatenate(o_parts, axis=1) if nb > 1 else o_parts[0]
            l = jnp.concatenate(l_parts, axis=1) if nb > 1 else l_parts[0]
            o_ref[:, :, lq] = o.reshape(nres, nb * blk, LANES).astype(BF16)
            l_ref[:, :, lq] = l.reshape(nres, nb * blk, LANES)


def _dilated(a1, a2, a3, batch):
    ins = [a1.reshape(batch, SEQ, A_QKV), a2, a3]
    in_specs = [pl.BlockSpec((d, SEQ // d, A_QKV), lambda b: (b, 0, 0)) for d in DILATIONS]
    out_specs = [pl.BlockSpec((d, SEQ // d, DIL_WIDTH), lambda b: (b, 0, 0)) for d in DILATIONS] * 2
    out_shape = ([jax.ShapeDtypeStruct((batch * d, SEQ // d, DIL_WIDTH), BF16) for d in DILATIONS]
                 + [jax.ShapeDtypeStruct((batch * d, SEQ // d, DIL_WIDTH), F32) for d in DILATIONS])
    return pl.pallas_call(
        _dilated_kernel,
        grid=(batch,),
        in_specs=in_specs,
        out_specs=out_specs,
        out_shape=out_shape,
        scratch_shapes=[pltpu.VMEM((N_DIL_GROUPS * DIL_HEADS, DIL_BLK, 2 * DIL_BLK), F32)],
        compiler_params=pltpu.CompilerParams(dimension_semantics=("arbitrary",), vmem_limit_bytes=VMEM_LIMIT),
        name="dilated",
    )(*ins)


def _moba_kernel(q_ref, k_ref, v_ref, o_ref, kaug_all, vaug_all, qaug_all, slope_ref, causal_ref):
    b = pl.program_id(0)
    p = pl.program_id(1)
    blk = MOBA_BLOCK

    @pl.when((b == 0) & (p == 0))
    def _():
        for h in range(MOBA_HEADS):
            slope_ref[h:h + 1, :] = jnp.full((1, SEQ), MOBA_SLOPES[h], F32)
        qi = lax.broadcasted_iota(jnp.int32, (blk, blk), 0)
        kj = lax.broadcasted_iota(jnp.int32, (blk, blk), 1)
        causal_ref[...] = jnp.where(qi >= kj, 0.0, NEG_INF)
        row_id = lax.broadcasted_iota(jnp.int32, (SEQ, LANES), 0)
        lane_id = lax.broadcasted_iota(jnp.int32, (SEQ, LANES), 1)
        onehot = jnp.where(row_id // blk == lane_id, 1.0, 0.0)
        offset = (row_id % blk).astype(F32)
        for sp in range(MOBA_PAIRS_PER_STEP):
            kaug_all[sp, :, LANES:] = jnp.where(lane_id == MOBA_NB, offset, onehot).astype(BF16)
            vaug_all[sp, :, LANES:] = jnp.ones((SEQ, LANES), BF16)

    m0, m1 = _head_masks()
    hms = _head_masks_bf16()
    scale = jnp.asarray(QK_SCALE, BF16)
    eye = jnp.where(lax.broadcasted_iota(jnp.int32, (blk, blk), 0)
                    == lax.broadcasted_iota(jnp.int32, (blk, blk), 1), 1.0, 0.0).astype(BF16)
    for sp in range(MOBA_PAIRS_PER_STEP):
        _moba_pair(sp, MOBA_PAIRS_PER_STEP * p + sp, q_ref, k_ref, v_ref, o_ref, kaug_all.at[sp],
                   vaug_all.at[sp], qaug_all.at[sp], slope_ref, causal_ref, m0, m1, hms, scale, eye)


def _moba_pair(sp, pair, q_ref, k_ref, v_ref, o_ref, kaug_ref, vaug_ref, qaug_ref, slope_ref, causal_ref,
               m0, m1, hms, scale, eye):
    blk = MOBA_BLOCK
    sl = slice(sp * LANES, (sp + 1) * LANES)
    k_all = k_ref[0, :, sl]
    kaug_ref[:, :LANES] = k_all
    vaug_ref[:, :LANES] = v_ref[0, :, sl]

    km = jnp.concatenate([jnp.mean(k_all[j * blk:(j + 1) * blk].astype(F32), axis=0, keepdims=True)
                          for j in range(MOBA_NB)], axis=0)

    q_all = q_ref[0, :, sl]
    ridx = lax.broadcasted_iota(jnp.int32, (MOBA_NB, SEQ), 0)
    qblk = lax.broadcasted_iota(jnp.int32, (MOBA_NB, SEQ), 1) // blk
    past = ridx < qblk
    km_terms = []
    for hm in (m0, m1):
        km_h = jnp.where(hm, km, 0.0)
        km_hi = km_h.astype(BF16).astype(F32)
        km_terms += [km_hi, km_h - km_hi]
    gates = _nt(jnp.concatenate(km_terms, axis=0).astype(BF16), q_all)
    for h in range(2):
        g0 = 2 * h * MOBA_NB
        gate = jnp.where(past, gates[g0:g0 + MOBA_NB] + gates[g0 + MOBA_NB:g0 + 2 * MOBA_NB], NEG_INF)
        cnt = jnp.zeros((MOBA_NB, SEQ), jnp.int32)
        for jp in range(MOBA_NB):
            beats = (gate[jp:jp + 1, :] > gate) | ((gate[jp:jp + 1, :] == gate) & (jp < ridx))
            cnt = cnt + beats.astype(jnp.int32)
        sel = ((cnt < MOBA_TOPK) & past) | (ridx == qblk)
        slope = slope_ref[pl.ds(2 * pair + h, 1), :]
        selb = jnp.where(sel, slope * ((ridx - qblk) * blk).astype(F32), NEG_INF)
        slope_row = jnp.where(ridx == 0, slope, 0.0)
        placed = jnp.concatenate([selb, slope_row, jnp.zeros((LANES - 2 * MOBA_NB, SEQ), F32)],
                                 axis=0).astype(BF16)
        q_h = q_all * (hms[h] * scale)
        for i in range(MOBA_NB):
            rows = slice(i * blk, (i + 1) * blk)
            qaug_ref[i, h * blk:(h + 1) * blk, :LANES] = q_h[rows]
            qaug_ref[i, h * blk:(h + 1) * blk, LANES:] = _nt(eye, placed[:, rows]).astype(BF16)

    def scores(i):
        nk = (i + 1) * blk
        s = _nt(qaug_ref[i], kaug_ref[0:nk, :])
        parts = []
        for h in range(2):
            sh = s[h * blk:(h + 1) * blk]
            s_own = sh[:, i * blk:] + causal_ref[...]
            m = jnp.max(s_own, axis=-1, keepdims=True)
            s_past = None
            if i > 0:
                s_past = sh[:, :i * blk]
                m = jnp.maximum(m, jnp.max(s_past, axis=-1, keepdims=True))
            parts.append((s_own, s_past, m))
        return parts

    def attend(i, parts):
        rows = slice(i * blk, (i + 1) * blk)
        pvs = []
        for s_own, s_past, m in parts:
            pv = _nn(jnp.exp(s_own - m).astype(BF16), vaug_ref[rows, :])
            if i > 0:
                pv = pv + _nn(jnp.exp(s_past - m).astype(BF16), vaug_ref[0:i * blk, :])
            pvs.append(pv)
        num = jnp.where(m0, pvs[0][:, :LANES], pvs[1][:, :LANES])
        den = jnp.where(m0, pvs[0][:, LANES:], pvs[1][:, LANES:])
        o_ref[0, rows, sl] = (num / den).astype(BF16)

    parts = scores(0)
    for i in range(1, MOBA_NB):
        nxt = scores(i)
        attend(i - 1, parts)
        parts = nxt
    attend(MOBA_NB - 1, parts)


def _moba(qkv_b, batch):
    v3 = qkv_b.reshape(batch, SEQ, B_QKV)
    pps = MOBA_PAIRS_PER_STEP
    nsteps = MOBA_HEADS // 2 // pps
    width = pps * LANES
    out = pl.pallas_call(
        _moba_kernel,
        grid=(batch, nsteps),
        in_specs=[pl.BlockSpec((1, SEQ, width), lambda b, p: (b, 0, p)),
                  pl.BlockSpec((1, SEQ, width), lambda b, p: (b, 0, nsteps + p)),
                  pl.BlockSpec((1, SEQ, width), lambda b, p: (b, 0, 2 * nsteps + p))],
        out_specs=pl.BlockSpec((1, SEQ, width), lambda b, p: (b, 0, p)),
        out_shape=jax.ShapeDtypeStruct((batch, SEQ, MOBA_WIDTH), BF16),
        scratch_shapes=[pltpu.VMEM((pps, SEQ, 2 * LANES), BF16),
                        pltpu.VMEM((pps, SEQ, 2 * LANES), BF16),
                        pltpu.VMEM((pps, MOBA_NB, 2 * MOBA_BLOCK, 2 * LANES), BF16),
                        pltpu.VMEM((MOBA_HEADS, SEQ), F32),
                        pltpu.VMEM((MOBA_BLOCK, MOBA_BLOCK), F32)],
        compiler_params=pltpu.CompilerParams(dimension_semantics=("arbitrary", "arbitrary"),
                                             vmem_limit_bytes=VMEM_LIMIT),
        name="moba",
    )(v3, v3, v3)
    return out.reshape(batch * SEQ, MOBA_WIDTH)


def _sigmoid(x):
    return 1.0 / (1.0 + jnp.exp(-x))


def _post_kernel(x_ref, o1_ref, o2_ref, o3_ref, l1_ref, l2_ref, l3_ref, ab_ref, g_ref, wzr_ref,
                 bm_ref, wpa_ref, wpb_ref, wout_ref, fg_ref, out_ref, o2s, o3s, l2s, l3s, *, final):
    for t in range(POST_SUBTILES):
        _post_tile(t, x_ref, o1_ref, o2_ref, o3_ref, l1_ref, l2_ref, l3_ref, ab_ref, g_ref, wzr_ref,
                   bm_ref, wpa_ref, wpb_ref, wout_ref, fg_ref, out_ref, o2s, o3s, l2s, l3s, final)


def _post_tile(t, x_ref, o1_ref, o2_ref, o3_ref, l1_ref, l2_ref, l3_ref, ab_ref, g_ref, wzr_ref,
               bm_ref, wpa_ref, wpb_ref, wout_ref, fg_ref, out_ref, o2s, o3s, l2s, l3s, final):
    rows = slice(t * ROW_TILE, (t + 1) * ROW_TILE)
    x = x_ref[rows, :]
    h = _rms(x, g_ref[0]).astype(BF16)

    def position_order(src, dst):
        d = src.shape[0]
        n = ROW_TILE // d
        for c in range(d):
            blk = src[c, t * n:(t + 1) * n, :].astype(F32)
            for j in range(DIL_WIDTH // LANES):
                dst[t, j, pl.ds(c, n, stride=d), :] = blk[:, j * LANES:(j + 1) * LANES]
        return jnp.concatenate([dst[t, j] for j in range(DIL_WIDTH // LANES)], axis=1)

    cw = DIL_WIDTH
    nchunk = D_MODEL // cw

    def gate_chunk(w0, b0, c):
        r = _nn(h, wzr_ref[0, :, w0 + c * cw:w0 + (c + 1) * cw]) + bm_ref[0, :, b0 + c * cw:b0 + (c + 1) * cw]
        return _sigmoid(r)

    z_a = _nn(h, wzr_ref[0, :, COL_ZA:COL_ZA + DIL_WIDTH])
    gb = []
    for c in range(MOBA_WIDTH // cw):
        z_b = _nn(h, wzr_ref[0, :, COL_ZB + c * cw:COL_ZB + (c + 1) * cw])
        gb.append((ab_ref[rows, c * cw:(c + 1) * cw].astype(F32) * (z_b * _sigmoid(z_b))).astype(BF16))
    gb = jnp.concatenate(gb, axis=1)

    o2, l2 = position_order(o2_ref, o2s), position_order(l2_ref, l2s)
    o3, l3 = position_order(o3_ref, o3s), position_order(l3_ref, l3s)
    l1 = l1_ref[rows, :]
    lmax = jnp.maximum(jnp.maximum(l1, l2), l3)
    w1, w2, w3 = jnp.exp(l1 - lmax), jnp.exp(l2 - lmax), jnp.exp(l3 - lmax)
    attn_a = (w1 * o1_ref[rows, :].astype(F32) + w2 * o2 + w3 * o3) / (w1 + w2 + w3)
    ga = (attn_a * (z_a * _sigmoid(z_a))).astype(BF16)

    gate_a = [gate_chunk(COL_RA, 0, c) for c in range(nchunk)]
    gate_b = [gate_chunk(COL_RB, D_MODEL, c) for c in range(nchunk)]
    merged = []
    for c in range(nchunk):
        cols = slice(c * cw, (c + 1) * cw)
        merged.append((gate_a[c] * _nn(ga, wpa_ref[0, :, cols])
                       + gate_b[c] * _nn(gb, wpb_ref[0, :, cols])).astype(BF16))
    xn = x + _nn(jnp.concatenate(merged, axis=1), wout_ref[0])
    if final:
        xn = _rms(xn, fg_ref[0])
    out_ref[rows, :] = xn


def _post(x2d, o, l, ab, g, w_in, bm, w_pa, w_pb, w_out, fg, layer):
    m = x2d.shape[0]
    tm = POST_SUBTILES * ROW_TILE
    per_b = SEQ // tm
    row = lambda w: pl.BlockSpec((tm, w), lambda i: (i, 0))
    res = lambda d: pl.BlockSpec((d, tm // d, DIL_WIDTH), lambda i: (i // per_b, i % per_b, 0))
    grp = [row(DIL_WIDTH), res(4), res(16)]
    params = (g, w_in, bm, w_pa, w_pb, w_out)
    return pl.pallas_call(
        functools.partial(_post_kernel, final=(layer == DEPTH - 1)),
        grid=(m // tm,),
        in_specs=([row(D_MODEL)] + grp + grp + [row(MOBA_WIDTH)] + [_layer_spec(a, layer) for a in params]
                  + [_layer_spec(fg, 0)]),
        out_specs=row(D_MODEL),
        out_shape=jax.ShapeDtypeStruct((m, D_MODEL), F32),
        scratch_shapes=[pltpu.VMEM((POST_SUBTILES, DIL_WIDTH // LANES, ROW_TILE, LANES), F32)] * 4,
        compiler_params=pltpu.CompilerParams(dimension_semantics=("arbitrary",), vmem_limit_bytes=VMEM_LIMIT),
        name="post",
    )(x2d, *o, *l, ab, *params, fg)


@jax.jit
def kernel(x, norm_g, w_in, b_merge, w_pa, w_pb, w_out, final_g):
    batch, seq, d = x.shape
    assert (seq, d) == (SEQ, D_MODEL)
    x2d = x.reshape(batch * seq, d)
    w_in, w_pa, w_pb, w_out = (w.astype(BF16) for w in (w_in, w_pa, w_pb, w_out))
    norm_g, b_merge, final_g = norm_g[:, None, :], b_merge[:, None, :], final_g[None, None, :]
    for layer in range(DEPTH):
        a1, a2, a3, qkv_b = _inproj(x2d, norm_g, w_in, layer, batch)
        outs = _dilated(a1, a2, a3, batch)
        ab = _moba(qkv_b, batch)
        o1 = outs[0].reshape(batch * seq, DIL_WIDTH)
        l1 = outs[3].reshape(batch * seq, DIL_WIDTH)
        x2d = _post(x2d, [o1, outs[1], outs[2]], [l1, outs[4], outs[5]], ab, norm_g, w_in, b_merge,
                    w_pa, w_pb, w_out, final_g, layer)
    return x2d.reshape(batch, seq, d)
```

```python
import functools

import numpy as np
import jax
import jax.numpy as jnp
from jax import lax
from jax.experimental import pallas as pl
from jax.experimental.pallas import tpu as pltpu

D_MODEL = 1024
SEQ = 2048
DEPTH = 2
HEAD_DIM = 64
DIL_CONFIGS = ((128, 1), (512, 4), (2048, 16))
DILATIONS = tuple(d for _, d in DIL_CONFIGS)
N_DIL_GROUPS = 3
DIL_HEADS = 4
DIL_WIDTH = DIL_HEADS * HEAD_DIM
DIL_BLK = 128
DIL_UNITS = SEQ // DIL_BLK
MOBA_HEADS = 8
MOBA_WIDTH = MOBA_HEADS * HEAD_DIM
MOBA_BLOCK = 256
MOBA_NB = SEQ // MOBA_BLOCK
MOBA_TOPK = 3
MOBA_PAIRS_PER_STEP = 2
RMS_EPS = 1e-6
NEG_INF = -1e30
QK_SCALE = HEAD_DIM ** -0.5

LANES = 128
A_QKV = 3 * DIL_WIDTH
B_QKV = 3 * MOBA_WIDTH
COL_ZA = N_DIL_GROUPS * A_QKV
COL_BQKV = COL_ZA + DIL_WIDTH
COL_ZB = COL_BQKV + B_QKV
COL_RA = COL_ZB + MOBA_WIDTH
COL_RB = COL_RA + D_MODEL
IN_COLS = COL_RB + D_MODEL
ROW_TILE = 512
INPROJ_SUBTILES = 2
POST_SUBTILES = 1
VMEM_LIMIT = 56 * 1024 * 1024

F32 = jnp.float32
BF16 = jnp.bfloat16


def _slopes(n):
    return [float(np.float32(2.0 ** (-8.0 * (i + 1) / n))) for i in range(n)]


DIL_SLOPES = _slopes(N_DIL_GROUPS * DIL_HEADS)
MOBA_SLOPES = _slopes(MOBA_HEADS)
assert all(np.frexp(s)[0] == 0.5 for s in MOBA_SLOPES)


def _nt(a, b):
    return lax.dot_general(a, b, (((1,), (1,)), ((), ())), preferred_element_type=F32)


def _nn(a, b):
    return jnp.dot(a, b, preferred_element_type=F32)


def _bnt(a, b):
    return lax.dot_general(a, b, (((2,), (2,)), ((0,), (0,))), preferred_element_type=F32)


def _bnn(a, b):
    return lax.dot_general(a, b, (((2,), (1,)), ((0,), (0,))), preferred_element_type=F32)


def _rms(x, g):
    ms = jnp.mean(x * x, axis=-1, keepdims=True)
    return x * lax.rsqrt(ms + RMS_EPS) * g


def _head_masks():
    lane = lax.broadcasted_iota(jnp.int32, (1, LANES), 1)
    return lane < HEAD_DIM, lane >= HEAD_DIM


def _head_masks_bf16():
    m0, m1 = _head_masks()
    return jnp.where(m0, 1.0, 0.0).astype(BF16), jnp.where(m1, 1.0, 0.0).astype(BF16)


def _inproj_kernel(x_ref, g_ref, w_ref, a1_ref, a2_ref, a3_ref, b_ref, h_ref):
    nchunk = D_MODEL // LANES
    for t in range(INPROJ_SUBTILES):
        rows = slice(t * ROW_TILE, (t + 1) * ROW_TILE)
        h = _rms(x_ref[rows, :], g_ref[0])
        hb = h.astype(BF16)
        a1_ref[rows, :] = _nn(hb, w_ref[0, :, 0:A_QKV]).astype(BF16)
        for c0 in range(0, B_QKV, A_QKV):
            b_ref[rows, c0:c0 + A_QKV] = _nn(
                hb, w_ref[0, :, COL_BQKV + c0:COL_BQKV + c0 + A_QKV]).astype(BF16)
        for j in range(nchunk):
            h_ref[t, j] = h[:, j * LANES:(j + 1) * LANES]
        for gi, o_ref in ((1, a2_ref), (2, a3_ref)):
            d = DILATIONS[gi]
            n = ROW_TILE // d
            hp = jnp.concatenate(
                [jnp.concatenate([h_ref[t, j, pl.ds(c, n, stride=d), :] for j in range(nchunk)], axis=1)
                 for c in range(d)], axis=0).astype(BF16)
            r = _nn(hp, w_ref[0, :, gi * A_QKV:(gi + 1) * A_QKV]).astype(BF16)
            for c in range(d):
                o_ref[c, t * n:(t + 1) * n, :] = r[c * n:(c + 1) * n]


def _layer_spec(a, layer):
    return pl.BlockSpec((1,) + a.shape[1:], lambda i: (layer, 0, 0), pipeline_mode=pl.Buffered(1))


def _inproj(x2d, g, w_in, layer, batch):
    m = x2d.shape[0]
    tm = INPROJ_SUBTILES * ROW_TILE
    per_b = SEQ // tm
    out_shape = [jax.ShapeDtypeStruct((m, A_QKV), BF16)]
    out_specs = [pl.BlockSpec((tm, A_QKV), lambda i: (i, 0))]
    for d in DILATIONS[1:]:
        out_shape.append(jax.ShapeDtypeStruct((batch * d, SEQ // d, A_QKV), BF16))
        out_specs.append(pl.BlockSpec((d, tm // d, A_QKV), lambda i: (i // per_b, i % per_b, 0)))
    out_shape.append(jax.ShapeDtypeStruct((m, B_QKV), BF16))
    out_specs.append(pl.BlockSpec((tm, B_QKV), lambda i: (i, 0)))
    return pl.pallas_call(
        _inproj_kernel,
        grid=(m // tm,),
        in_specs=[pl.BlockSpec((tm, D_MODEL), lambda i: (i, 0)), _layer_spec(g, layer), _layer_spec(w_in, layer)],
        out_specs=out_specs,
        out_shape=out_shape,
        scratch_shapes=[pltpu.VMEM((INPROJ_SUBTILES, D_MODEL // LANES, ROW_TILE, LANES), F32)],
        compiler_params=pltpu.CompilerParams(dimension_semantics=("arbitrary",), vmem_limit_bytes=VMEM_LIMIT),
        name="inproj",
    )(x2d, g, w_in)


def _band_attend(q, kk, vv, bias_ref, g, pair, with_prev):
    m0, _ = _head_masks()
    hms = _head_masks_bf16()
    scale = jnp.asarray(QK_SCALE, BF16)
    pvs, ms = [], []
    vv1 = jnp.concatenate([vv, jnp.ones_like(vv)], axis=2)
    for h in range(2):
        idx = g * DIL_HEADS + pair * 2 + h
        bias = bias_ref[idx] if with_prev else bias_ref[idx, :, DIL_BLK:]
        s = _bnt(q * (hms[h] * scale), kk) + bias
        m = jnp.max(s, axis=-1, keepdims=True)
        pvs.append(_bnn(jnp.exp(s - m).astype(BF16), vv1))
        ms.append(m)
    num = jnp.where(m0, pvs[0][:, :, :LANES], pvs[1][:, :, :LANES])
    den = jnp.where(m0, pvs[0][:, :, LANES:], pvs[1][:, :, LANES:])
    return num / den, jnp.where(m0, ms[0], ms[1]) + jnp.log(den)


def _dilated_kernel(a1_ref, a2_ref, a3_ref, o1_ref, o2_ref, o3_ref, l1_ref, l2_ref, l3_ref, bias_ref):
    @pl.when(pl.program_id(0) == 0)
    def _():
        qi = lax.broadcasted_iota(jnp.int32, (DIL_BLK, 2 * DIL_BLK), 0)
        kj = lax.broadcasted_iota(jnp.int32, (DIL_BLK, 2 * DIL_BLK), 1)
        rel = DIL_BLK + qi - kj
        valid = (rel >= 0) & (rel <= DIL_BLK)
        for g, d in enumerate(DILATIONS):
            for h in range(DIL_HEADS):
                idx = g * DIL_HEADS + h
                bias_ref[idx] = jnp.where(valid, -DIL_SLOPES[idx] * (rel * d).astype(F32), NEG_INF)

    blk = DIL_BLK
    groups = ((a1_ref, o1_ref, l1_ref, 1), (a2_ref, o2_ref, l2_ref, 4), (a3_ref, o3_ref, l3_ref, 16))
    for g, (a_ref, o_ref, l_ref, nres) in enumerate(groups):
        nb = DIL_UNITS // nres
        for pair in range(2):
            lq = slice(pair * LANES, (pair + 1) * LANES)
            lk = slice(DIL_WIDTH + pair * LANES, DIL_WIDTH + (pair + 1) * LANES)
            lv = slice(2 * DIL_WIDTH + pair * LANES, 2 * DIL_WIDTH + (pair + 1) * LANES)

            def blocks(lanes):
                return a_ref[:, :, lanes].reshape(nres, nb, blk, LANES)

            q, k, v = blocks(lq), blocks(lk), blocks(lv)
            o_first, l_first = _band_attend(q[:, 0], k[:, 0], v[:, 0], bias_ref, g, pair, False)
            o_parts, l_parts = [o_first[:, None]], [l_first[:, None]]
            if nb > 1:
                e = nres * (nb - 1)
                kk = jnp.concatenate([k[:, :-1], k[:, 1:]], axis=2).reshape(e, 2 * blk, LANES)
                vv = jnp.concatenate([v[:, :-1], v[:, 1:]], axis=2).reshape(e, 2 * blk, LANES)
                o_rest, l_rest = _band_attend(q[:, 1:].reshape(e, blk, LANES), kk, vv, bias_ref, g, pair, True)
                o_parts.append(o_rest.reshape(nres, nb - 1, blk, LANES))
                l_parts.append(l_rest.reshape(nres, nb - 1, blk, LANES))
            o = jnp.concatenate(o_parts, axis=1) if nb > 1 else o_parts[0]
            l = jnp.concatenate(l_parts, axis=1) if nb > 1 else l_parts[0]
            o_ref[:, :, lq] = o.reshape(nres, nb * blk, LANES).astype(BF16)
            l_ref[:, :, lq] = l.reshape(nres, nb * blk, LANES)


def _dilated(a1, a2, a3, batch):
    ins = [a1.reshape(batch, SEQ, A_QKV), a2, a3]
    in_specs = [pl.BlockSpec((d, SEQ // d, A_QKV), lambda b: (b, 0, 0)) for d in DILATIONS]
    out_specs = [pl.BlockSpec((d, SEQ // d, DIL_WIDTH), lambda b: (b, 0, 0)) for d in DILATIONS] * 2
    out_shape = ([jax.ShapeDtypeStruct((batch * d, SEQ // d, DIL_WIDTH), BF16) for d in DILATIONS]
                 + [jax.ShapeDtypeStruct((batch * d, SEQ // d, DIL_WIDTH), F32) for d in DILATIONS])
    return pl.pallas_call(
        _dilated_kernel,
        grid=(batch,),
        in_specs=in_specs,
        out_specs=out_specs,
        out_shape=out_shape,
        scratch_shapes=[pltpu.VMEM((N_DIL_GROUPS * DIL_HEADS, DIL_BLK, 2 * DIL_BLK), F32)],
        compiler_params=pltpu.CompilerParams(dimension_semantics=("arbitrary",), vmem_limit_bytes=VMEM_LIMIT),
        name="dilated",
    )(*ins)


def _moba_kernel(q_ref, k_ref, v_ref, o_ref, kaug_all, vaug_all, qaug_all, slope_ref, causal_ref):
    b = pl.program_id(0)
    p = pl.program_id(1)
    blk = MOBA_BLOCK

    @pl.when((b == 0) & (p == 0))
    def _():
        for h in range(MOBA_HEADS):
            slope_ref[h:h + 1, :] = jnp.full((1, SEQ), MOBA_SLOPES[h], F32)
        qi = lax.broadcasted_iota(jnp.int32, (blk, blk), 0)
        kj = lax.broadcasted_iota(jnp.int32, (blk, blk), 1)
        causal_ref[...] = jnp.where(qi >= kj, 0.0, NEG_INF)
        row_id = lax.broadcasted_iota(jnp.int32, (SEQ, LANES), 0)
        lane_id = lax.broadcasted_iota(jnp.int32, (SEQ, LANES), 1)
        onehot = jnp.where(row_id // blk == lane_id, 1.0, 0.0)
        offset = (row_id % blk).astype(F32)
        for sp in range(MOBA_PAIRS_PER_STEP):
            kaug_all[sp, :, LANES:] = jnp.where(lane_id == MOBA_NB, offset, onehot).astype(BF16)
            vaug_all[sp, :, LANES:] = jnp.ones((SEQ, LANES), BF16)

    m0, m1 = _head_masks()
    hms = _head_masks_bf16()
    scale = jnp.asarray(QK_SCALE, BF16)
    eye = jnp.where(lax.broadcasted_iota(jnp.int32, (blk, blk), 0)
                    == lax.broadcasted_iota(jnp.int32, (blk, blk), 1), 1.0, 0.0).astype(BF16)
    for sp in range(MOBA_PAIRS_PER_STEP):
        _moba_pair(sp, MOBA_PAIRS_PER_STEP * p + sp, q_ref, k_ref, v_ref, o_ref, kaug_all.at[sp],
                   vaug_all.at[sp], qaug_all.at[sp], slope_ref, causal_ref, m0, m1, hms, scale, eye)


def _moba_pair(sp, pair, q_ref, k_ref, v_ref, o_ref, kaug_ref, vaug_ref, qaug_ref, slope_ref, causal_ref,
               m0, m1, hms, scale, eye):
    blk = MOBA_BLOCK
    sl = slice(sp * LANES, (sp + 1) * LANES)
    k_all = k_ref[0, :, sl]
    kaug_ref[:, :LANES] = k_all
    vaug_ref[:, :LANES] = v_ref[0, :, sl]

    km = jnp.concatenate([jnp.mean(k_all[j * blk:(j + 1) * blk].astype(F32), axis=0, keepdims=True)
                          for j in range(MOBA_NB)], axis=0)

    q_all = q_ref[0, :, sl]
    ridx = lax.broadcasted_iota(jnp.int32, (MOBA_NB, SEQ), 0)
    qblk = lax.broadcasted_iota(jnp.int32, (MOBA_NB, SEQ), 1) // blk
    past = ridx < qblk
    km_terms = []
    for hm in (m0, m1):
        km_h = jnp.where(hm, km, 0.0)
        km_hi = km_h.astype(BF16).astype(F32)
        km_terms += [km_hi, km_h - km_hi]
    gates = _nt(jnp.concatenate(km_terms, axis=0).astype(BF16), q_all)
    for h in range(2):
        g0 = 2 * h * MOBA_NB
        gate = jnp.where(past, gates[g0:g0 + MOBA_NB] + gates[g0 + MOBA_NB:g0 + 2 * MOBA_NB], NEG_INF)
        cnt = jnp.zeros((MOBA_NB, SEQ), jnp.int32)
        for jp in range(MOBA_NB):
            beats = (gate[jp:jp + 1, :] > gate) | ((gate[jp:jp + 1, :] == gate) & (jp < ridx))
            cnt = cnt + beats.astype(jnp.int32)
        sel = ((cnt < MOBA_TOPK) & past) | (ridx == qblk)
        slope = slope_ref[pl.ds(2 * pair + h, 1), :]
        selb = jnp.where(sel, slope * ((ridx - qblk) * blk).astype(F32), NEG_INF)
        slope_row = jnp.where(ridx == 0, slope, 0.0)
        placed = jnp.concatenate([selb, slope_row, jnp.zeros((LANES - 2 * MOBA_NB, SEQ), F32)],
                                 axis=0).astype(BF16)
        q_h = q_all * (hms[h] * scale)
        for i in range(MOBA_NB):
            rows = slice(i * blk, (i + 1) * blk)
            qaug_ref[i, h * blk:(h + 1) * blk, :LANES] = q_h[rows]
            qaug_ref[i, h * blk:(h + 1) * blk, LANES:] = _nt(eye, placed[:, rows]).astype(BF16)

    def scores(i):
        nk = (i + 1) * blk
        s = _nt(qaug_ref[i], kaug_ref[0:nk, :])
        parts = []
        for h in range(2):
            sh = s[h * blk:(h + 1) * blk]
            s_own = sh[:, i * blk:] + causal_ref[...]
            m = jnp.max(s_own, axis=-1, keepdims=True)
            s_past = None
            if i > 0:
                s_past = sh[:, :i * blk]
                m = jnp.maximum(m, jnp.max(s_past, axis=-1, keepdims=True))
            parts.append((s_own, s_past, m))
        return parts

    def attend(i, parts):
        rows = slice(i * blk, (i + 1) * blk)
        pvs = []
        for s_own, s_past, m in parts:
            pv = _nn(jnp.exp(s_own - m).astype(BF16), vaug_ref[rows, :])
            if i > 0:
                pv = pv + _nn(jnp.exp(s_past - m).astype(BF16), vaug_ref[0:i * blk, :])
            pvs.append(pv)
        num = jnp.where(m0, pvs[0][:, :LANES], pvs[1][:, :LANES])
        den = jnp.where(m0, pvs[0][:, LANES:], pvs[1][:, LANES:])
        o_ref[0, rows, sl] = (num / den).astype(BF16)

    order = list(range(MOBA_NB))
    if sp == MOBA_PAIRS_PER_STEP - 1:
        order.reverse()
    parts = scores(order[0])
    for cur, nxt in zip(order[:-1], order[1:]):
        nxt_parts = scores(nxt)
        attend(cur, parts)
        parts = nxt_parts
    attend(order[-1], parts)


def _moba(qkv_b, batch):
    v3 = qkv_b.reshape(batch, SEQ, B_QKV)
    pps = MOBA_PAIRS_PER_STEP
    nsteps = MOBA_HEADS // 2 // pps
    width = pps * LANES
    out = pl.pallas_call(
        _moba_kernel,
        grid=(batch, nsteps),
        in_specs=[pl.BlockSpec((1, SEQ, width), lambda b, p: (b, 0, p)),
                  pl.BlockSpec((1, SEQ, width), lambda b, p: (b, 0, nsteps + p)),
                  pl.BlockSpec((1, SEQ, width), lambda b, p: (b, 0, 2 * nsteps + p))],
        out_specs=pl.BlockSpec((1, SEQ, width), lambda b, p: (b, 0, p)),
        out_shape=jax.ShapeDtypeStruct((batch, SEQ, MOBA_WIDTH), BF16),
        scratch_shapes=[pltpu.VMEM((pps, SEQ, 2 * LANES), BF16),
                        pltpu.VMEM((pps, SEQ, 2 * LANES), BF16),
                        pltpu.VMEM((pps, MOBA_NB, 2 * MOBA_BLOCK, 2 * LANES), BF16),
                        pltpu.VMEM((MOBA_HEADS, SEQ), F32),
                        pltpu.VMEM((MOBA_BLOCK, MOBA_BLOCK), F32)],
        compiler_params=pltpu.CompilerParams(dimension_semantics=("arbitrary", "arbitrary"),
                                             vmem_limit_bytes=VMEM_LIMIT),
        name="moba",
    )(v3, v3, v3)
    return out.reshape(batch * SEQ, MOBA_WIDTH)


def _sigmoid(x):
    return 1.0 / (1.0 + jnp.exp(-x))


def _post_kernel(x_ref, o1_ref, o2_ref, o3_ref, l1_ref, l2_ref, l3_ref, ab_ref, g_ref, wzr_ref,
                 bm_ref, wpa_ref, wpb_ref, wout_ref, fg_ref, out_ref, o2s, o3s, l2s, l3s, *, final):
    for t in range(POST_SUBTILES):
        _post_tile(t, x_ref, o1_ref, o2_ref, o3_ref, l1_ref, l2_ref, l3_ref, ab_ref, g_ref, wzr_ref,
                   bm_ref, wpa_ref, wpb_ref, wout_ref, fg_ref, out_ref, o2s, o3s, l2s, l3s, final)


def _post_tile(t, x_ref, o1_ref, o2_ref, o3_ref, l1_ref, l2_ref, l3_ref, ab_ref, g_ref, wzr_ref,
               bm_ref, wpa_ref, wpb_ref, wout_ref, fg_ref, out_ref, o2s, o3s, l2s, l3s, final):
    rows = slice(t * ROW_TILE, (t + 1) * ROW_TILE)
    x = x_ref[rows, :]
    h = _rms(x, g_ref[0]).astype(BF16)

    def position_order(src, dst):
        d = src.shape[0]
        n = ROW_TILE // d
        for c in range(d):
            blk = src[c, t * n:(t + 1) * n, :].astype(F32)
            for j in range(DIL_WIDTH // LANES):
                dst[t, j, pl.ds(c, n, stride=d), :] = blk[:, j * LANES:(j + 1) * LANES]
        return jnp.concatenate([dst[t, j] for j in range(DIL_WIDTH // LANES)], axis=1)

    cw = DIL_WIDTH
    nchunk = D_MODEL // cw

    def gate_chunk(w0, b0, c):
        r = _nn(h, wzr_ref[0, :, w0 + c * cw:w0 + (c + 1) * cw]) + bm_ref[0, :, b0 + c * cw:b0 + (c + 1) * cw]
        return _sigmoid(r)

    z_a = _nn(h, wzr_ref[0, :, COL_ZA:COL_ZA + DIL_WIDTH])
    gb = []
    for c in range(MOBA_WIDTH // cw):
        z_b = _nn(h, wzr_ref[0, :, COL_ZB + c * cw:COL_ZB + (c + 1) * cw])
        gb.append((ab_ref[rows, c * cw:(c + 1) * cw].astype(F32) * (z_b * _sigmoid(z_b))).astype(BF16))
    gb = jnp.concatenate(gb, axis=1)
    gate_a = [gate_chunk(COL_RA, 0, c) for c in range(nchunk)]

    o2, l2 = position_order(o2_ref, o2s), position_order(l2_ref, l2s)
    o3, l3 = position_order(o3_ref, o3s), position_order(l3_ref, l3s)
    l1 = l1_ref[rows, :]
    lmax = jnp.maximum(jnp.maximum(l1, l2), l3)
    w1, w2, w3 = jnp.exp(l1 - lmax), jnp.exp(l2 - lmax), jnp.exp(l3 - lmax)
    attn_a = (w1 * o1_ref[rows, :].astype(F32) + w2 * o2 + w3 * o3) / (w1 + w2 + w3)
    ga = (attn_a * (z_a * _sigmoid(z_a))).astype(BF16)

    gate_b = [gate_chunk(COL_RB, D_MODEL, c) for c in range(nchunk)]
    merged = []
    for c in range(nchunk):
        cols = slice(c * cw, (c + 1) * cw)
        merged.append((gate_a[c] * _nn(ga, wpa_ref[0, :, cols])
                       + gate_b[c] * _nn(gb, wpb_ref[0, :, cols])).astype(BF16))
    xn = x + _nn(jnp.concatenate(merged, axis=1), wout_ref[0])
    if final:
        xn = _rms(xn, fg_ref[0])
    out_ref[rows, :] = xn


def _post(x2d, o, l, ab, g, w_in, bm, w_pa, w_pb, w_out, fg, layer):
    m = x2d.shape[0]
    tm = POST_SUBTILES * ROW_TILE
    per_b = SEQ // tm
    row = lambda w: pl.BlockSpec((tm, w), lambda i: (i, 0))
    res = lambda d: pl.BlockSpec((d, tm // d, DIL_WIDTH), lambda i: (i // per_b, i % per_b, 0))
    grp = [row(DIL_WIDTH), res(4), res(16)]
    params = (g, w_in, bm, w_pa, w_pb, w_out)
    return pl.pallas_call(
        functools.partial(_post_kernel, final=(layer == DEPTH - 1)),
        grid=(m // tm,),
        in_specs=([row(D_MODEL)] + grp + grp + [row(MOBA_WIDTH)] + [_layer_spec(a, layer) for a in params]
                  + [_layer_spec(fg, 0)]),
        out_specs=row(D_MODEL),
        out_shape=jax.ShapeDtypeStruct((m, D_MODEL), F32),
        scratch_shapes=[pltpu.VMEM((POST_SUBTILES, DIL_WIDTH // LANES, ROW_TILE, LANES), F32)] * 4,
        compiler_params=pltpu.CompilerParams(dimension_semantics=("arbitrary",), vmem_limit_bytes=VMEM_LIMIT),
        name="post",
    )(x2d, *o, *l, ab, *params, fg)


@jax.jit
def kernel(x, norm_g, w_in, b_merge, w_pa, w_pb, w_out, final_g):
    batch, seq, d = x.shape
    assert (seq, d) == (SEQ, D_MODEL)
    x2d = x.reshape(batch * seq, d)
    w_in, w_pa, w_pb, w_out = (w.astype(BF16) for w in (w_in, w_pa, w_pb, w_out))
    norm_g, b_merge, final_g = norm_g[:, None, :], b_merge[:, None, :], final_g[None, None, :]
    for layer in range(DEPTH):
        a1, a2, a3, qkv_b = _inproj(x2d, norm_g, w_in, layer, batch)
        outs = _dilated(a1, a2, a3, batch)
        ab = _moba(qkv_b, batch)
        o1 = outs[0].reshape(batch * seq, DIL_WIDTH)
        l1 = outs[3].reshape(batch * seq, DIL_WIDTH)
        x2d = _post(x2d, [o1, outs[1], outs[2]], [l1, outs[4], outs[5]], ab, norm_g, w_in, b_merge,
                    w_pa, w_pb, w_out, final_g, layer)
    return x2d.reshape(batch, seq, d)
```

```python
import functools

import numpy as np
import jax
import jax.numpy as jnp
from jax import lax
from jax.experimental import pallas as pl
from jax.experimental.pallas import tpu as pltpu

D_MODEL = 1024
SEQ = 2048
DEPTH = 2
HEAD_DIM = 64
DIL_CONFIGS = ((128, 1), (512, 4), (2048, 16))
DILATIONS = tuple(d for _, d in DIL_CONFIGS)
N_DIL_GROUPS = 3
DIL_HEADS = 4
DIL_WIDTH = DIL_HEADS * HEAD_DIM
DIL_BLK = 128
DIL_UNITS = SEQ // DIL_BLK
MOBA_HEADS = 8
MOBA_WIDTH = MOBA_HEADS * HEAD_DIM
MOBA_BLOCK = 256
MOBA_NB = SEQ // MOBA_BLOCK
MOBA_TOPK = 3
MOBA_PAIRS_PER_STEP = 2
RMS_EPS = 1e-6
NEG_INF = -1e30
QK_SCALE = HEAD_DIM ** -0.5

LANES = 128
A_QKV = 3 * DIL_WIDTH
B_QKV = 3 * MOBA_WIDTH
COL_ZA = N_DIL_GROUPS * A_QKV
COL_BQKV = COL_ZA + DIL_WIDTH
COL_ZB = COL_BQKV + B_QKV
COL_RA = COL_ZB + MOBA_WIDTH
COL_RB = COL_RA + D_MODEL
IN_COLS = COL_RB + D_MODEL
ROW_TILE = 512
INPROJ_SUBTILES = 2
VMEM_LIMIT = 56 * 1024 * 1024

F32 = jnp.float32
BF16 = jnp.bfloat16


def _slopes(n):
    return [float(np.float32(2.0 ** (-8.0 * (i + 1) / n))) for i in range(n)]


DIL_SLOPES = _slopes(N_DIL_GROUPS * DIL_HEADS)
MOBA_SLOPES = _slopes(MOBA_HEADS)
assert all(np.frexp(s)[0] == 0.5 for s in MOBA_SLOPES)


def _nt(a, b):
    return lax.dot_general(a, b, (((1,), (1,)), ((), ())), preferred_element_type=F32)


def _nn(a, b):
    return jnp.dot(a, b, preferred_element_type=F32)


def _bnt(a, b):
    return lax.dot_general(a, b, (((2,), (2,)), ((0,), (0,))), preferred_element_type=F32)


def _bnn(a, b):
    return lax.dot_general(a, b, (((2,), (1,)), ((0,), (0,))), preferred_element_type=F32)


def _rms(x, g):
    ms = jnp.mean(x * x, axis=-1, keepdims=True)
    return x * lax.rsqrt(ms + RMS_EPS) * g


def _head_masks():
    lane = lax.broadcasted_iota(jnp.int32, (1, LANES), 1)
    return lane < HEAD_DIM, lane >= HEAD_DIM


def _head_masks_bf16():
    m0, m1 = _head_masks()
    return jnp.where(m0, 1.0, 0.0).astype(BF16), jnp.where(m1, 1.0, 0.0).astype(BF16)


def _inproj_kernel(x_ref, g_ref, w_ref, a1_ref, a2_ref, a3_ref, b_ref, h_ref):
    nchunk = D_MODEL // LANES
    for t in range(INPROJ_SUBTILES):
        rows = slice(t * ROW_TILE, (t + 1) * ROW_TILE)
        h = _rms(x_ref[rows, :], g_ref[0])
        hb = h.astype(BF16)
        a1_ref[rows, :] = _nn(hb, w_ref[0, :, 0:A_QKV]).astype(BF16)
        for c0 in range(0, B_QKV, A_QKV):
            b_ref[rows, c0:c0 + A_QKV] = _nn(
                hb, w_ref[0, :, COL_BQKV + c0:COL_BQKV + c0 + A_QKV]).astype(BF16)
        for j in range(nchunk):
            h_ref[t, j] = h[:, j * LANES:(j + 1) * LANES]
        for gi, o_ref in ((1, a2_ref), (2, a3_ref)):
            d = DILATIONS[gi]
            n = ROW_TILE // d
            hp = jnp.concatenate(
                [jnp.concatenate([h_ref[t, j, pl.ds(c, n, stride=d), :] for j in range(nchunk)], axis=1)
                 for c in range(d)], axis=0).astype(BF16)
            r = _nn(hp, w_ref[0, :, gi * A_QKV:(gi + 1) * A_QKV]).astype(BF16)
            for c in range(d):
                o_ref[c, t * n:(t + 1) * n, :] = r[c * n:(c + 1) * n]


def _layer_spec(a, layer):
    return pl.BlockSpec((1,) + a.shape[1:], lambda i: (layer, 0, 0), pipeline_mode=pl.Buffered(1))


def _inproj(x2d, g, w_in, layer, batch):
    m = x2d.shape[0]
    tm = INPROJ_SUBTILES * ROW_TILE
    per_b = SEQ // tm
    out_shape = [jax.ShapeDtypeStruct((m, A_QKV), BF16)]
    out_specs = [pl.BlockSpec((tm, A_QKV), lambda i: (i, 0))]
    for d in DILATIONS[1:]:
        out_shape.append(jax.ShapeDtypeStruct((batch * d, SEQ // d, A_QKV), BF16))
        out_specs.append(pl.BlockSpec((d, tm // d, A_QKV), lambda i: (i // per_b, i % per_b, 0)))
    out_shape.append(jax.ShapeDtypeStruct((m, B_QKV), BF16))
    out_specs.append(pl.BlockSpec((tm, B_QKV), lambda i: (i, 0)))
    return pl.pallas_call(
        _inproj_kernel,
        grid=(m // tm,),
        in_specs=[pl.BlockSpec((tm, D_MODEL), lambda i: (i, 0)), _layer_spec(g, layer), _layer_spec(w_in, layer)],
        out_specs=out_specs,
        out_shape=out_shape,
        scratch_shapes=[pltpu.VMEM((INPROJ_SUBTILES, D_MODEL // LANES, ROW_TILE, LANES), F32)],
        compiler_params=pltpu.CompilerParams(dimension_semantics=("arbitrary",), vmem_limit_bytes=VMEM_LIMIT),
        name="inproj",
    )(x2d, g, w_in)


def _band_attend(q, kk, vv, bias_ref, g, pair, with_prev):
    m0, _ = _head_masks()
    hms = _head_masks_bf16()
    scale = jnp.asarray(QK_SCALE, BF16)
    pvs, ms = [], []
    vv1 = jnp.concatenate([vv, jnp.ones_like(vv)], axis=2)
    for h in range(2):
        idx = g * DIL_HEADS + pair * 2 + h
        bias = bias_ref[idx] if with_prev else bias_ref[idx, :, DIL_BLK:]
        s = _bnt(q * (hms[h] * scale), kk) + bias
        m = jnp.max(s, axis=-1, keepdims=True)
        pvs.append(_bnn(jnp.exp(s - m).astype(BF16), vv1))
        ms.append(m)
    num = jnp.where(m0, pvs[0][:, :, :LANES], pvs[1][:, :, :LANES])
    den = jnp.where(m0, pvs[0][:, :, LANES:], pvs[1][:, :, LANES:])
    return num / den, jnp.where(m0, ms[0], ms[1]) + jnp.log(den)


def _dilated_kernel(a1_ref, a2_ref, a3_ref, o1_ref, o2_ref, o3_ref, l1_ref, l2_ref, l3_ref, bias_ref):
    @pl.when(pl.program_id(0) == 0)
    def _():
        qi = lax.broadcasted_iota(jnp.int32, (DIL_BLK, 2 * DIL_BLK), 0)
        kj = lax.broadcasted_iota(jnp.int32, (DIL_BLK, 2 * DIL_BLK), 1)
        rel = DIL_BLK + qi - kj
        valid = (rel >= 0) & (rel <= DIL_BLK)
        for g, d in enumerate(DILATIONS):
            for h in range(DIL_HEADS):
                idx = g * DIL_HEADS + h
                bias_ref[idx] = jnp.where(valid, -DIL_SLOPES[idx] * (rel * d).astype(F32), NEG_INF)

    blk = DIL_BLK
    groups = ((a1_ref, o1_ref, l1_ref, 1), (a2_ref, o2_ref, l2_ref, 4), (a3_ref, o3_ref, l3_ref, 16))
    for g, (a_ref, o_ref, l_ref, nres) in enumerate(groups):
        nb = DIL_UNITS // nres
        for pair in range(2):
            lq = slice(pair * LANES, (pair + 1) * LANES)
            lk = slice(DIL_WIDTH + pair * LANES, DIL_WIDTH + (pair + 1) * LANES)
            lv = slice(2 * DIL_WIDTH + pair * LANES, 2 * DIL_WIDTH + (pair + 1) * LANES)

            def blocks(lanes):
                return a_ref[:, :, lanes].reshape(nres, nb, blk, LANES)

            q, k, v = blocks(lq), blocks(lk), blocks(lv)
            o_first, l_first = _band_attend(q[:, 0], k[:, 0], v[:, 0], bias_ref, g, pair, False)
            o_parts, l_parts = [o_first[:, None]], [l_first[:, None]]
            if nb > 1:
                e = nres * (nb - 1)
                kk = jnp.concatenate([k[:, :-1], k[:, 1:]], axis=2).reshape(e, 2 * blk, LANES)
                vv = jnp.concatenate([v[:, :-1], v[:, 1:]], axis=2).reshape(e, 2 * blk, LANES)
                o_rest, l_rest = _band_attend(q[:, 1:].reshape(e, blk, LANES), kk, vv, bias_ref, g, pair, True)
                o_parts.append(o_rest.reshape(nres, nb - 1, blk, LANES))
                l_parts.append(l_rest.reshape(nres, nb - 1, blk, LANES))
            o = jnp.concatenate(o_parts, axis=1) if nb > 1 else o_parts[0]
            l = jnp.concatenate(l_parts, axis=1) if nb > 1 else l_parts[0]
            o_ref[:, :, lq] = o.reshape(nres, nb * blk, LANES).astype(BF16)
            l_ref[:, :, lq] = l.reshape(nres, nb * blk, LANES)


def _dilated(a1, a2, a3, batch):
    ins = [a1.reshape(batch, SEQ, A_QKV), a2, a3]
    in_specs = [pl.BlockSpec((d, SEQ // d, A_QKV), lambda b: (b, 0, 0)) for d in DILATIONS]
    out_specs = [pl.BlockSpec((d, SEQ // d, DIL_WIDTH), lambda b: (b, 0, 0)) for d in DILATIONS] * 2
    out_shape = ([jax.ShapeDtypeStruct((batch * d, SEQ // d, DIL_WIDTH), BF16) for d in DILATIONS]
                 + [jax.ShapeDtypeStruct((batch * d, SEQ // d, DIL_WIDTH), F32) for d in DILATIONS])
    return pl.pallas_call(
        _dilated_kernel,
        grid=(batch,),
        in_specs=in_specs,
        out_specs=out_specs,
        out_shape=out_shape,
        scratch_shapes=[pltpu.VMEM((N_DIL_GROUPS * DIL_HEADS, DIL_BLK, 2 * DIL_BLK), F32)],
        compiler_params=pltpu.CompilerParams(dimension_semantics=("arbitrary",), vmem_limit_bytes=VMEM_LIMIT),
        name="dilated",
    )(*ins)


def _moba_kernel(q_ref, k_ref, v_ref, o_ref, kaug_all, vaug_all, qaug_all, slope_ref, causal_ref):
    b = pl.program_id(0)
    p = pl.program_id(1)
    blk = MOBA_BLOCK

    @pl.when((b == 0) & (p == 0))
    def _():
        for h in range(MOBA_HEADS):
            slope_ref[h:h + 1, :] = jnp.full((1, SEQ), MOBA_SLOPES[h], F32)
        qi = lax.broadcasted_iota(jnp.int32, (blk, blk), 0)
        kj = lax.broadcasted_iota(jnp.int32, (blk, blk), 1)
        causal_ref[...] = jnp.where(qi >= kj, 0.0, NEG_INF)
        row_id = lax.broadcasted_iota(jnp.int32, (SEQ, LANES), 0)
        lane_id = lax.broadcasted_iota(jnp.int32, (SEQ, LANES), 1)
        onehot = jnp.where(row_id // blk == lane_id, 1.0, 0.0)
        offset = (row_id % blk).astype(F32)
        for sp in range(MOBA_PAIRS_PER_STEP):
            kaug_all[sp, :, LANES:] = jnp.where(lane_id == MOBA_NB, offset, onehot).astype(BF16)
            vaug_all[sp, :, LANES:] = jnp.ones((SEQ, LANES), BF16)

    m0, m1 = _head_masks()
    hms = _head_masks_bf16()
    scale = jnp.asarray(QK_SCALE, BF16)
    for sp in range(MOBA_PAIRS_PER_STEP):
        _moba_pair(sp, MOBA_PAIRS_PER_STEP * p + sp, q_ref, k_ref, v_ref, o_ref, kaug_all.at[sp],
                   vaug_all.at[sp], qaug_all.at[sp], slope_ref, causal_ref, m0, m1, hms, scale)


def _moba_pair(sp, pair, q_ref, k_ref, v_ref, o_ref, kaug_ref, vaug_ref, qaug_ref, slope_ref, causal_ref,
               m0, m1, hms, scale):
    blk = MOBA_BLOCK
    sl = slice(sp * LANES, (sp + 1) * LANES)
    k_all = k_ref[0, :, sl]
    kaug_ref[:, :LANES] = k_all
    vaug_ref[:, :LANES] = v_ref[0, :, sl]

    km = jnp.concatenate([jnp.mean(k_all[j * blk:(j + 1) * blk].astype(F32), axis=0, keepdims=True)
                          for j in range(MOBA_NB)], axis=0)

    q_all = q_ref[0, :, sl]
    ridx = lax.broadcasted_iota(jnp.int32, (MOBA_NB, SEQ), 0)
    qblk = lax.broadcasted_iota(jnp.int32, (MOBA_NB, SEQ), 1) // blk
    past = ridx < qblk
    km_terms = []
    for hm in (m0, m1):
        km_h = jnp.where(hm, km, 0.0)
        km_hi = km_h.astype(BF16).astype(F32)
        km_terms += [km_hi, km_h - km_hi]
    gates = _nt(jnp.concatenate(km_terms, axis=0).astype(BF16), q_all)
    for h in range(2):
        g0 = 2 * h * MOBA_NB
        gate = jnp.where(past, gates[g0:g0 + MOBA_NB] + gates[g0 + MOBA_NB:g0 + 2 * MOBA_NB], NEG_INF)
        cnt = jnp.zeros((MOBA_NB, SEQ), jnp.int32)
        for jp in range(MOBA_NB):
            beats = (gate[jp:jp + 1, :] > gate) | ((gate[jp:jp + 1, :] == gate) & (jp < ridx))
            cnt = cnt + beats.astype(jnp.int32)
        sel = ((cnt < MOBA_TOPK) & past) | (ridx == qblk)
        slope = slope_ref[pl.ds(2 * pair + h, 1), :]
        selb = jnp.where(sel, slope * ((ridx - qblk) * blk).astype(F32), NEG_INF)
        slope_row = jnp.where(ridx == 0, slope, 0.0)
        placed = jnp.concatenate([selb, slope_row, jnp.zeros((LANES - 2 * MOBA_NB, SEQ), F32)], axis=0)
        q_h = q_all * (hms[h] * scale)
        for i in range(MOBA_NB):
            rows = slice(i * blk, (i + 1) * blk)
            qaug_ref[i, h * blk:(h + 1) * blk, :LANES] = q_h[rows]
            qaug_ref[i, h * blk:(h + 1) * blk, LANES:] = placed[:, rows].T.astype(BF16)

    def scores(i):
        nk = (i + 1) * blk
        s = _nt(qaug_ref[i], kaug_ref[0:nk, :])
        parts = []
        for h in range(2):
            sh = s[h * blk:(h + 1) * blk]
            s_own = sh[:, i * blk:] + causal_ref[...]
            m = jnp.max(s_own, axis=-1, keepdims=True)
            s_past = None
            if i > 0:
                s_past = sh[:, :i * blk]
                m = jnp.maximum(m, jnp.max(s_past, axis=-1, keepdims=True))
            parts.append((s_own, s_past, m))
        return parts

    def attend(i, parts):
        rows = slice(i * blk, (i + 1) * blk)
        pvs = []
        for s_own, s_past, m in parts:
            pv = _nn(jnp.exp(s_own - m).astype(BF16), vaug_ref[rows, :])
            if i > 0:
                pv = pv + _nn(jnp.exp(s_past - m).astype(BF16), vaug_ref[0:i * blk, :])
            pvs.append(pv)
        num = jnp.where(m0, pvs[0][:, :LANES], pvs[1][:, :LANES])
        den = jnp.where(m0, pvs[0][:, LANES:], pvs[1][:, LANES:])
        o_ref[0, rows, sl] = (num / den).astype(BF16)

    parts = scores(0)
    for i in range(1, MOBA_NB):
        nxt = scores(i)
        attend(i - 1, parts)
        parts = nxt
    attend(MOBA_NB - 1, parts)


def _moba(qkv_b, batch):
    v3 = qkv_b.reshape(batch, SEQ, B_QKV)
    pps = MOBA_PAIRS_PER_STEP
    nsteps = MOBA_HEADS // 2 // pps
    width = pps * LANES
    out = pl.pallas_call(
        _moba_kernel,
        grid=(batch, nsteps),
        in_specs=[pl.BlockSpec((1, SEQ, width), lambda b, p: (b, 0, p)),
                  pl.BlockSpec((1, SEQ, width), lambda b, p: (b, 0, nsteps + p)),
                  pl.BlockSpec((1, SEQ, width), lambda b, p: (b, 0, 2 * nsteps + p))],
        out_specs=pl.BlockSpec((1, SEQ, width), lambda b, p: (b, 0, p)),
        out_shape=jax.ShapeDtypeStruct((batch, SEQ, MOBA_WIDTH), BF16),
        scratch_shapes=[pltpu.VMEM((pps, SEQ, 2 * LANES), BF16),
                        pltpu.VMEM((pps, SEQ, 2 * LANES), BF16),
                        pltpu.VMEM((pps, MOBA_NB, 2 * MOBA_BLOCK, 2 * LANES), BF16),
                        pltpu.VMEM((MOBA_HEADS, SEQ), F32),
                        pltpu.VMEM((MOBA_BLOCK, MOBA_BLOCK), F32)],
        compiler_params=pltpu.CompilerParams(dimension_semantics=("arbitrary", "arbitrary"),
                                             vmem_limit_bytes=VMEM_LIMIT),
        name="moba",
    )(v3, v3, v3)
    return out.reshape(batch * SEQ, MOBA_WIDTH)


def _sigmoid(x):
    return 1.0 / (1.0 + jnp.exp(-x))


def _post_kernel(x_ref, o1_ref, o2_ref, o3_ref, l1_ref, l2_ref, l3_ref, ab_ref, g_ref, wzr_ref,
                 bm_ref, wpa_ref, wpb_ref, wout_ref, fg_ref, out_ref, o2s, o3s, l2s, l3s, *, final):
    x = x_ref[...]
    h = _rms(x, g_ref[0]).astype(BF16)

    def position_order(src, dst):
        d, n = src.shape[0], src.shape[1]
        for c in range(d):
            blk = src[c].astype(F32)
            for j in range(DIL_WIDTH // LANES):
                dst[j, pl.ds(c, n, stride=d), :] = blk[:, j * LANES:(j + 1) * LANES]
        return jnp.concatenate([dst[j] for j in range(DIL_WIDTH // LANES)], axis=1)

    cw = DIL_WIDTH
    nchunk = D_MODEL // cw

    def gate_chunk(w0, b0, c):
        r = _nn(h, wzr_ref[0, :, w0 + c * cw:w0 + (c + 1) * cw]) + bm_ref[0, :, b0 + c * cw:b0 + (c + 1) * cw]
        return _sigmoid(r)

    z_a = _nn(h, wzr_ref[0, :, COL_ZA:COL_ZA + DIL_WIDTH])
    gb = []
    for c in range(MOBA_WIDTH // cw):
        z_b = _nn(h, wzr_ref[0, :, COL_ZB + c * cw:COL_ZB + (c + 1) * cw])
        gb.append((ab_ref[:, c * cw:(c + 1) * cw].astype(F32) * (z_b * _sigmoid(z_b))).astype(BF16))
    gb = jnp.concatenate(gb, axis=1)

    o2, l2 = position_order(o2_ref, o2s), position_order(l2_ref, l2s)
    o3, l3 = position_order(o3_ref, o3s), position_order(l3_ref, l3s)
    l1 = l1_ref[...]
    lmax = jnp.maximum(jnp.maximum(l1, l2), l3)
    w1, w2, w3 = jnp.exp(l1 - lmax), jnp.exp(l2 - lmax), jnp.exp(l3 - lmax)
    attn_a = (w1 * o1_ref[...].astype(F32) + w2 * o2 + w3 * o3) / (w1 + w2 + w3)
    ga = (attn_a * (z_a * _sigmoid(z_a))).astype(BF16)

    gate_a = [gate_chunk(COL_RA, 0, c) for c in range(nchunk)]
    gate_b = [gate_chunk(COL_RB, D_MODEL, c) for c in range(nchunk)]
    merged = []
    for c in range(nchunk):
        cols = slice(c * cw, (c + 1) * cw)
        merged.append((gate_a[c] * _nn(ga, wpa_ref[0, :, cols])
                       + gate_b[c] * _nn(gb, wpb_ref[0, :, cols])).astype(BF16))
    xn = x + _nn(jnp.concatenate(merged, axis=1), wout_ref[0])
    if final:
        xn = _rms(xn, fg_ref[0])
    out_ref[...] = xn


def _post(x2d, o, l, ab, g, w_in, bm, w_pa, w_pb, w_out, fg, layer):
    m = x2d.shape[0]
    tm = ROW_TILE
    per_b = SEQ // tm
    row = lambda w: pl.BlockSpec((tm, w), lambda i: (i, 0))
    res = lambda d: pl.BlockSpec((d, tm // d, DIL_WIDTH), lambda i: (i // per_b, i % per_b, 0))
    grp = [row(DIL_WIDTH), res(4), res(16)]
    params = (g, w_in, bm, w_pa, w_pb, w_out)
    return pl.pallas_call(
        functools.partial(_post_kernel, final=(layer == DEPTH - 1)),
        grid=(m // tm,),
        in_specs=([row(D_MODEL)] + grp + grp + [row(MOBA_WIDTH)] + [_layer_spec(a, layer) for a in params]
                  + [_layer_spec(fg, 0)]),
        out_specs=row(D_MODEL),
        out_shape=jax.ShapeDtypeStruct((m, D_MODEL), F32),
        scratch_shapes=[pltpu.VMEM((DIL_WIDTH // LANES, tm, LANES), F32)] * 4,
        compiler_params=pltpu.CompilerParams(dimension_semantics=("arbitrary",), vmem_limit_bytes=VMEM_LIMIT),
        name="post",
    )(x2d, *o, *l, ab, *params, fg)


@jax.jit
def kernel(x, norm_g, w_in, b_merge, w_pa, w_pb, w_out, final_g):
    batch, seq, d = x.shape
    assert (seq, d) == (SEQ, D_MODEL)
    x2d = x.reshape(batch * seq, d)
    w_in, w_pa, w_pb, w_out = (w.astype(BF16) for w in (w_in, w_pa, w_pb, w_out))
    norm_g, b_merge, final_g = norm_g[:, None, :], b_merge[:, None, :], final_g[None, None, :]
    for layer in range(DEPTH):
        a1, a2, a3, qkv_b = _inproj(x2d, norm_g, w_in, layer, batch)
        outs = _dilated(a1, a2, a3, batch)
        ab = _moba(qkv_b, batch)
        o1 = outs[0].reshape(batch * seq, DIL_WIDTH)
        l1 = outs[3].reshape(batch * seq, DIL_WIDTH)
        x2d = _post(x2d, [o1, outs[1], outs[2]], [l1, outs[4], outs[5]], ab, norm_g, w_in, b_merge,
                    w_pa, w_pb, w_out, final_g, layer)
    return x2d.reshape(batch, seq, d)
```

```python
import functools

import numpy as np
import jax
import jax.numpy as jnp
from jax import lax
from jax.experimental import pallas as pl
from jax.experimental.pallas import tpu as pltpu

D_MODEL = 1024
SEQ = 2048
DEPTH = 2
HEAD_DIM = 64
DIL_CONFIGS = ((128, 1), (512, 4), (2048, 16))
DILATIONS = tuple(d for _, d in DIL_CONFIGS)
N_DIL_GROUPS = 3
DIL_HEADS = 4
DIL_WIDTH = DIL_HEADS * HEAD_DIM
DIL_BLK = 128
DIL_UNITS = SEQ // DIL_BLK
DIL_RING = 3
MOBA_HEADS = 8
MOBA_WIDTH = MOBA_HEADS * HEAD_DIM
MOBA_BLOCK = 256
MOBA_NB = SEQ // MOBA_BLOCK
MOBA_TOPK = 3
MOBA_PAIRS_PER_STEP = 2
RMS_EPS = 1e-6
NEG_INF = -1e30
QK_SCALE = HEAD_DIM ** -0.5

LANES = 128
A_QKV = 3 * DIL_WIDTH
B_QKV = 3 * MOBA_WIDTH
COL_ZA = N_DIL_GROUPS * A_QKV
COL_BQKV = COL_ZA + DIL_WIDTH
COL_ZB = COL_BQKV + B_QKV
COL_RA = COL_ZB + MOBA_WIDTH
COL_RB = COL_RA + D_MODEL
IN_COLS = COL_RB + D_MODEL
ROW_TILE = 512
INPROJ_SUBTILES = 2
VMEM_LIMIT = 56 * 1024 * 1024

F32 = jnp.float32
BF16 = jnp.bfloat16


def _slopes(n):
    return [float(np.float32(2.0 ** (-8.0 * (i + 1) / n))) for i in range(n)]


DIL_SLOPES = _slopes(N_DIL_GROUPS * DIL_HEADS)
MOBA_SLOPES = _slopes(MOBA_HEADS)
assert all(np.frexp(s)[0] == 0.5 for s in MOBA_SLOPES)


def _nt(a, b):
    return lax.dot_general(a, b, (((1,), (1,)), ((), ())), preferred_element_type=F32)


def _nn(a, b):
    return jnp.dot(a, b, preferred_element_type=F32)


def _bnt(a, b):
    return lax.dot_general(a, b, (((2,), (2,)), ((0,), (0,))), preferred_element_type=F32)


def _bnn(a, b):
    return lax.dot_general(a, b, (((2,), (1,)), ((0,), (0,))), preferred_element_type=F32)


def _rms(x, g):
    ms = jnp.mean(x * x, axis=-1, keepdims=True)
    return x * lax.rsqrt(ms + RMS_EPS) * g


def _head_masks():
    lane = lax.broadcasted_iota(jnp.int32, (1, LANES), 1)
    return lane < HEAD_DIM, lane >= HEAD_DIM


def _head_masks_bf16():
    m0, m1 = _head_masks()
    return jnp.where(m0, 1.0, 0.0).astype(BF16), jnp.where(m1, 1.0, 0.0).astype(BF16)


def _inproj_kernel(x_ref, g_ref, w_ref, a1_ref, a2_ref, a3_ref, b_ref, h_ref):
    nchunk = D_MODEL // LANES
    for t in range(INPROJ_SUBTILES):
        rows = slice(t * ROW_TILE, (t + 1) * ROW_TILE)
        h = _rms(x_ref[rows, :], g_ref[0])
        hb = h.astype(BF16)
        a1_ref[rows, :] = _nn(hb, w_ref[0, :, 0:A_QKV]).astype(BF16)
        for c0 in range(0, B_QKV, A_QKV):
            b_ref[rows, c0:c0 + A_QKV] = _nn(
                hb, w_ref[0, :, COL_BQKV + c0:COL_BQKV + c0 + A_QKV]).astype(BF16)
        for j in range(nchunk):
            h_ref[t, j] = h[:, j * LANES:(j + 1) * LANES]
        for gi, o_ref in ((1, a2_ref), (2, a3_ref)):
            d = DILATIONS[gi]
            n = ROW_TILE // d
            hp = jnp.concatenate(
                [jnp.concatenate([h_ref[t, j, pl.ds(c, n, stride=d), :] for j in range(nchunk)], axis=1)
                 for c in range(d)], axis=0).astype(BF16)
            r = _nn(hp, w_ref[0, :, gi * A_QKV:(gi + 1) * A_QKV]).astype(BF16)
            for c in range(d):
                o_ref[c, t * n:(t + 1) * n, :] = r[c * n:(c + 1) * n]


def _layer_spec(a, layer):
    return pl.BlockSpec((1,) + a.shape[1:], lambda i: (layer, 0, 0), pipeline_mode=pl.Buffered(1))


def _inproj(x2d, g, w_in, layer, batch):
    m = x2d.shape[0]
    tm = INPROJ_SUBTILES * ROW_TILE
    per_b = SEQ // tm
    out_shape = [jax.ShapeDtypeStruct((m, A_QKV), BF16)]
    out_specs = [pl.BlockSpec((tm, A_QKV), lambda i: (i, 0))]
    for d in DILATIONS[1:]:
        out_shape.append(jax.ShapeDtypeStruct((batch * d, SEQ // d, A_QKV), BF16))
        out_specs.append(pl.BlockSpec((d, tm // d, A_QKV), lambda i: (i // per_b, i % per_b, 0)))
    out_shape.append(jax.ShapeDtypeStruct((m, B_QKV), BF16))
    out_specs.append(pl.BlockSpec((tm, B_QKV), lambda i: (i, 0)))
    return pl.pallas_call(
        _inproj_kernel,
        grid=(m // tm,),
        in_specs=[pl.BlockSpec((tm, D_MODEL), lambda i: (i, 0)), _layer_spec(g, layer), _layer_spec(w_in, layer)],
        out_specs=out_specs,
        out_shape=out_shape,
        scratch_shapes=[pltpu.VMEM((INPROJ_SUBTILES, D_MODEL // LANES, ROW_TILE, LANES), F32)],
        compiler_params=pltpu.CompilerParams(dimension_semantics=("arbitrary",), vmem_limit_bytes=VMEM_LIMIT),
        name="inproj",
    )(x2d, g, w_in)


def _band_attend(q, kk, vv, bias_ref, g, pair, with_prev):
    m0, _ = _head_masks()
    hms = _head_masks_bf16()
    scale = jnp.asarray(QK_SCALE, BF16)
    pvs, ms = [], []
    vv1 = jnp.concatenate([vv, jnp.ones_like(vv)], axis=2)
    for h in range(2):
        idx = g * DIL_HEADS + pair * 2 + h
        bias = bias_ref[idx] if with_prev else bias_ref[idx, :, DIL_BLK:]
        s = _bnt(q * (hms[h] * scale), kk) + bias
        m = jnp.max(s, axis=-1, keepdims=True)
        pvs.append(_bnn(jnp.exp(s - m).astype(BF16), vv1))
        ms.append(m)
    num = jnp.where(m0, pvs[0][:, :, :LANES], pvs[1][:, :, :LANES])
    den = jnp.where(m0, pvs[0][:, :, LANES:], pvs[1][:, :, LANES:])
    return num / den, jnp.where(m0, ms[0], ms[1]) + jnp.log(den)


def _dilated_kernel(a1_hbm, a2_hbm, a3_hbm, o1_ref, o2_ref, o3_ref, l1_ref, l2_ref, l3_ref, bias_ref,
                    buf1, buf2, buf3, sem):
    b = pl.program_id(0)
    nsteps = pl.num_programs(0)

    def fetch(step, slot):
        return [pltpu.make_async_copy(hbm.at[pl.ds(step * d, d)], buf.at[slot], sem.at[gi, slot])
                for gi, (hbm, buf, d) in enumerate(((a1_hbm, buf1, 1), (a2_hbm, buf2, 4), (a3_hbm, buf3, 16)))]

    @pl.when(b == 0)
    def _():
        for ahead in range(DIL_RING - 1):
            for cp in fetch(ahead, ahead):
                cp.start()

    @pl.when(b + DIL_RING - 1 < nsteps)
    def _():
        for cp in fetch(b + DIL_RING - 1, (b + DIL_RING - 1) % DIL_RING):
            cp.start()

    slot = b % DIL_RING
    for cp in fetch(b, slot):
        cp.wait()
    a1_ref, a2_ref, a3_ref = buf1.at[slot], buf2.at[slot], buf3.at[slot]

    @pl.when(b == 0)
    def _():
        qi = lax.broadcasted_iota(jnp.int32, (DIL_BLK, 2 * DIL_BLK), 0)
        kj = lax.broadcasted_iota(jnp.int32, (DIL_BLK, 2 * DIL_BLK), 1)
        rel = DIL_BLK + qi - kj
        valid = (rel >= 0) & (rel <= DIL_BLK)
        for g, d in enumerate(DILATIONS):
            for h in range(DIL_HEADS):
                idx = g * DIL_HEADS + h
                bias_ref[idx] = jnp.where(valid, -DIL_SLOPES[idx] * (rel * d).astype(F32), NEG_INF)

    blk = DIL_BLK
    groups = ((a1_ref, o1_ref, l1_ref, 1), (a2_ref, o2_ref, l2_ref, 4), (a3_ref, o3_ref, l3_ref, 16))
    for g, (a_ref, o_ref, l_ref, nres) in enumerate(groups):
        nb = DIL_UNITS // nres
        for pair in range(2):
            lq = slice(pair * LANES, (pair + 1) * LANES)
            lk = slice(DIL_WIDTH + pair * LANES, DIL_WIDTH + (pair + 1) * LANES)
            lv = slice(2 * DIL_WIDTH + pair * LANES, 2 * DIL_WIDTH + (pair + 1) * LANES)

            def blocks(lanes):
                return a_ref[:, :, lanes].reshape(nres, nb, blk, LANES)

            q, k, v = blocks(lq), blocks(lk), blocks(lv)
            o_first, l_first = _band_attend(q[:, 0], k[:, 0], v[:, 0], bias_ref, g, pair, False)
            o_parts, l_parts = [o_first[:, None]], [l_first[:, None]]
            if nb > 1:
                e = nres * (nb - 1)
                kk = jnp.concatenate([k[:, :-1], k[:, 1:]], axis=2).reshape(e, 2 * blk, LANES)
                vv = jnp.concatenate([v[:, :-1], v[:, 1:]], axis=2).reshape(e, 2 * blk, LANES)
                o_rest, l_rest = _band_attend(q[:, 1:].reshape(e, blk, LANES), kk, vv, bias_ref, g, pair, True)
                o_parts.append(o_rest.reshape(nres, nb - 1, blk, LANES))
                l_parts.append(l_rest.reshape(nres, nb - 1, blk, LANES))
            o = jnp.concatenate(o_parts, axis=1) if nb > 1 else o_parts[0]
            l = jnp.concatenate(l_parts, axis=1) if nb > 1 else l_parts[0]
            o_ref[:, :, lq] = o.reshape(nres, nb * blk, LANES).astype(BF16)
            l_ref[:, :, lq] = l.reshape(nres, nb * blk, LANES)


def _dilated(a1, a2, a3, batch):
    ins = [a1.reshape(batch, SEQ, A_QKV), a2, a3]
    assert batch >= DIL_RING
    in_specs = [pl.BlockSpec(memory_space=pl.ANY)] * N_DIL_GROUPS
    out_specs = [pl.BlockSpec((d, SEQ // d, DIL_WIDTH), lambda b: (b, 0, 0)) for d in DILATIONS] * 2
    out_shape = ([jax.ShapeDtypeStruct((batch * d, SEQ // d, DIL_WIDTH), BF16) for d in DILATIONS]
                 + [jax.ShapeDtypeStruct((batch * d, SEQ // d, DIL_WIDTH), F32) for d in DILATIONS])
    return pl.pallas_call(
        _dilated_kernel,
        grid=(batch,),
        in_specs=in_specs,
        out_specs=out_specs,
        out_shape=out_shape,
        scratch_shapes=([pltpu.VMEM((N_DIL_GROUPS * DIL_HEADS, DIL_BLK, 2 * DIL_BLK), F32)]
                        + [pltpu.VMEM((DIL_RING, d, SEQ // d, A_QKV), BF16) for d in DILATIONS]
                        + [pltpu.SemaphoreType.DMA((N_DIL_GROUPS, DIL_RING))]),
        compiler_params=pltpu.CompilerParams(dimension_semantics=("arbitrary",), vmem_limit_bytes=VMEM_LIMIT),
        name="dilated",
    )(*ins)


def _moba_kernel(q_ref, k_ref, v_ref, o_ref, kaug_all, vaug_all, qaug_all, slope_ref, causal_ref):
    b = pl.program_id(0)
    p = pl.program_id(1)
    blk = MOBA_BLOCK

    @pl.when((b == 0) & (p == 0))
    def _():
        for h in range(MOBA_HEADS):
            slope_ref[h:h + 1, :] = jnp.full((1, SEQ), MOBA_SLOPES[h], F32)
        qi = lax.broadcasted_iota(jnp.int32, (blk, blk), 0)
        kj = lax.broadcasted_iota(jnp.int32, (blk, blk), 1)
        causal_ref[...] = jnp.where(qi >= kj, 0.0, NEG_INF)
        row_id = lax.broadcasted_iota(jnp.int32, (SEQ, LANES), 0)
        lane_id = lax.broadcasted_iota(jnp.int32, (SEQ, LANES), 1)
        onehot = jnp.where(row_id // blk == lane_id, 1.0, 0.0)
        offset = (row_id % blk).astype(F32)
        for sp in range(MOBA_PAIRS_PER_STEP):
            kaug_all[sp, :, LANES:] = jnp.where(lane_id == MOBA_NB, offset, onehot).astype(BF16)
            vaug_all[sp, :, LANES:] = jnp.ones((SEQ, LANES), BF16)

    m0, m1 = _head_masks()
    hms = _head_masks_bf16()
    scale = jnp.asarray(QK_SCALE, BF16)
    for sp in range(MOBA_PAIRS_PER_STEP):
        _moba_pair(sp, MOBA_PAIRS_PER_STEP * p + sp, q_ref, k_ref, v_ref, o_ref, kaug_all.at[sp],
                   vaug_all.at[sp], qaug_all.at[sp], slope_ref, causal_ref, m0, m1, hms, scale)


def _moba_pair(sp, pair, q_ref, k_ref, v_ref, o_ref, kaug_ref, vaug_ref, qaug_ref, slope_ref, causal_ref,
               m0, m1, hms, scale):
    blk = MOBA_BLOCK
    sl = slice(sp * LANES, (sp + 1) * LANES)
    k_all = k_ref[0, :, sl]
    kaug_ref[:, :LANES] = k_all
    vaug_ref[:, :LANES] = v_ref[0, :, sl]

    km = jnp.concatenate([jnp.mean(k_all[j * blk:(j + 1) * blk].astype(F32), axis=0, keepdims=True)
                          for j in range(MOBA_NB)], axis=0)

    q_all = q_ref[0, :, sl]
    ridx = lax.broadcasted_iota(jnp.int32, (MOBA_NB, SEQ), 0)
    qblk = lax.broadcasted_iota(jnp.int32, (MOBA_NB, SEQ), 1) // blk
    past = ridx < qblk
    km_terms = []
    for hm in (m0, m1):
        km_h = jnp.where(hm, km, 0.0)
        km_hi = km_h.astype(BF16).astype(F32)
        km_terms += [km_hi, km_h - km_hi]
    gates = _nt(jnp.concatenate(km_terms, axis=0).astype(BF16), q_all)
    for h in range(2):
        g0 = 2 * h * MOBA_NB
        gate = jnp.where(past, gates[g0:g0 + MOBA_NB] + gates[g0 + MOBA_NB:g0 + 2 * MOBA_NB], NEG_INF)
        cnt = jnp.zeros((MOBA_NB, SEQ), jnp.int32)
        for jp in range(MOBA_NB):
            beats = (gate[jp:jp + 1, :] > gate) | ((gate[jp:jp + 1, :] == gate) & (jp < ridx))
            cnt = cnt + beats.astype(jnp.int32)
        sel = ((cnt < MOBA_TOPK) & past) | (ridx == qblk)
        slope = slope_ref[pl.ds(2 * pair + h, 1), :]
        selb = jnp.where(sel, slope * ((ridx - qblk) * blk).astype(F32), NEG_INF)
        slope_row = jnp.where(ridx == 0, slope, 0.0)
        placed = jnp.concatenate([selb, slope_row, jnp.zeros((LANES - 2 * MOBA_NB, SEQ), F32)], axis=0)
        q_h = q_all * (hms[h] * scale)
        for i in range(MOBA_NB):
            rows = slice(i * blk, (i + 1) * blk)
            qaug_ref[i, h * blk:(h + 1) * blk, :LANES] = q_h[rows]
            qaug_ref[i, h * blk:(h + 1) * blk, LANES:] = placed[:, rows].T.astype(BF16)

    def scores(i):
        nk = (i + 1) * blk
        s = _nt(qaug_ref[i], kaug_ref[0:nk, :])
        parts = []
        for h in range(2):
            sh = s[h * blk:(h + 1) * blk]
            s_own = sh[:, i * blk:] + causal_ref[...]
            m = jnp.max(s_own, axis=-1, keepdims=True)
            s_past = None
            if i > 0:
                s_past = sh[:, :i * blk]
                m = jnp.maximum(m, jnp.max(s_past, axis=-1, keepdims=True))
            parts.append((s_own, s_past, m))
        return parts

    def attend(i, parts):
        rows = slice(i * blk, (i + 1) * blk)
        pvs = []
        for s_own, s_past, m in parts:
            pv = _nn(jnp.exp(s_own - m).astype(BF16), vaug_ref[rows, :])
            if i > 0:
                pv = pv + _nn(jnp.exp(s_past - m).astype(BF16), vaug_ref[0:i * blk, :])
            pvs.append(pv)
        num = jnp.where(m0, pvs[0][:, :LANES], pvs[1][:, :LANES])
        den = jnp.where(m0, pvs[0][:, LANES:], pvs[1][:, LANES:])
        o_ref[0, rows, sl] = (num / den).astype(BF16)

    parts = scores(0)
    for i in range(1, MOBA_NB):
        nxt = scores(i)
        attend(i - 1, parts)
        parts = nxt
    attend(MOBA_NB - 1, parts)


def _moba(qkv_b, batch):
    v3 = qkv_b.reshape(batch, SEQ, B_QKV)
    pps = MOBA_PAIRS_PER_STEP
    nsteps = MOBA_HEADS // 2 // pps
    width = pps * LANES
    out = pl.pallas_call(
        _moba_kernel,
        grid=(batch, nsteps),
        in_specs=[pl.BlockSpec((1, SEQ, width), lambda b, p: (b, 0, p)),
                  pl.BlockSpec((1, SEQ, width), lambda b, p: (b, 0, nsteps + p)),
                  pl.BlockSpec((1, SEQ, width), lambda b, p: (b, 0, 2 * nsteps + p))],
        out_specs=pl.BlockSpec((1, SEQ, width), lambda b, p: (b, 0, p)),
        out_shape=jax.ShapeDtypeStruct((batch, SEQ, MOBA_WIDTH), BF16),
        scratch_shapes=[pltpu.VMEM((pps, SEQ, 2 * LANES), BF16),
                        pltpu.VMEM((pps, SEQ, 2 * LANES), BF16),
                        pltpu.VMEM((pps, MOBA_NB, 2 * MOBA_BLOCK, 2 * LANES), BF16),
                        pltpu.VMEM((MOBA_HEADS, SEQ), F32),
                        pltpu.VMEM((MOBA_BLOCK, MOBA_BLOCK), F32)],
        compiler_params=pltpu.CompilerParams(dimension_semantics=("arbitrary", "arbitrary"),
                                             vmem_limit_bytes=VMEM_LIMIT),
        name="moba",
    )(v3, v3, v3)
    return out.reshape(batch * SEQ, MOBA_WIDTH)


def _sigmoid(x):
    return 1.0 / (1.0 + jnp.exp(-x))


def _post_kernel(x_ref, o1_ref, o2_ref, o3_ref, l1_ref, l2_ref, l3_ref, ab_ref, g_ref, wzr_ref,
                 bm_ref, wpa_ref, wpb_ref, wout_ref, fg_ref, out_ref, o2s, o3s, l2s, l3s, *, final):
    x = x_ref[...]
    h = _rms(x, g_ref[0]).astype(BF16)

    def position_order(src, dst):
        d, n = src.shape[0], src.shape[1]
        for c in range(d):
            blk = src[c].astype(F32)
            for j in range(DIL_WIDTH // LANES):
                dst[j, pl.ds(c, n, stride=d), :] = blk[:, j * LANES:(j + 1) * LANES]
        return jnp.concatenate([dst[j] for j in range(DIL_WIDTH // LANES)], axis=1)

    cw = DIL_WIDTH
    nchunk = D_MODEL // cw

    def gate_chunk(w0, b0, c):
        r = _nn(h, wzr_ref[0, :, w0 + c * cw:w0 + (c + 1) * cw]) + bm_ref[0, :, b0 + c * cw:b0 + (c + 1) * cw]
        return _sigmoid(r)

    z_a = _nn(h, wzr_ref[0, :, COL_ZA:COL_ZA + DIL_WIDTH])
    gb = []
    for c in range(MOBA_WIDTH // cw):
        z_b = _nn(h, wzr_ref[0, :, COL_ZB + c * cw:COL_ZB + (c + 1) * cw])
        gb.append((ab_ref[:, c * cw:(c + 1) * cw].astype(F32) * (z_b * _sigmoid(z_b))).astype(BF16))
    gb = jnp.concatenate(gb, axis=1)

    o2, l2 = position_order(o2_ref, o2s), position_order(l2_ref, l2s)
    o3, l3 = position_order(o3_ref, o3s), position_order(l3_ref, l3s)
    l1 = l1_ref[...]
    lmax = jnp.maximum(jnp.maximum(l1, l2), l3)
    w1, w2, w3 = jnp.exp(l1 - lmax), jnp.exp(l2 - lmax), jnp.exp(l3 - lmax)
    attn_a = (w1 * o1_ref[...].astype(F32) + w2 * o2 + w3 * o3) / (w1 + w2 + w3)
    ga = (attn_a * (z_a * _sigmoid(z_a))).astype(BF16)

    gate_a = [gate_chunk(COL_RA, 0, c) for c in range(nchunk)]
    gate_b = [gate_chunk(COL_RB, D_MODEL, c) for c in range(nchunk)]
    merged = []
    for c in range(nchunk):
        cols = slice(c * cw, (c + 1) * cw)
        merged.append((gate_a[c] * _nn(ga, wpa_ref[0, :, cols])
                       + gate_b[c] * _nn(gb, wpb_ref[0, :, cols])).astype(BF16))
    xn = x + _nn(jnp.concatenate(merged, axis=1), wout_ref[0])
    if final:
        xn = _rms(xn, fg_ref[0])
    out_ref[...] = xn


def _post(x2d, o, l, ab, g, w_in, bm, w_pa, w_pb, w_out, fg, layer):
    m = x2d.shape[0]
    tm = ROW_TILE
    per_b = SEQ // tm
    row = lambda w: pl.BlockSpec((tm, w), lambda i: (i, 0))
    res = lambda d: pl.BlockSpec((d, tm // d, DIL_WIDTH), lambda i: (i // per_b, i % per_b, 0))
    grp = [row(DIL_WIDTH), res(4), res(16)]
    params = (g, w_in, bm, w_pa, w_pb, w_out)
    return pl.pallas_call(
        functools.partial(_post_kernel, final=(layer == DEPTH - 1)),
        grid=(m // tm,),
        in_specs=([row(D_MODEL)] + grp + grp + [row(MOBA_WIDTH)] + [_layer_spec(a, layer) for a in params]
                  + [_layer_spec(fg, 0)]),
        out_specs=row(D_MODEL),
        out_shape=jax.ShapeDtypeStruct((m, D_MODEL), F32),
        scratch_shapes=[pltpu.VMEM((DIL_WIDTH // LANES, tm, LANES), F32)] * 4,
        compiler_params=pltpu.CompilerParams(dimension_semantics=("arbitrary",), vmem_limit_bytes=VMEM_LIMIT),
        name="post",
    )(x2d, *o, *l, ab, *params, fg)


@jax.jit
def kernel(x, norm_g, w_in, b_merge, w_pa, w_pb, w_out, final_g):
    batch, seq, d = x.shape
    assert (seq, d) == (SEQ, D_MODEL)
    x2d = x.reshape(batch * seq, d)
    w_in, w_pa, w_pb, w_out = (w.astype(BF16) for w in (w_in, w_pa, w_pb, w_out))
    norm_g, b_merge, final_g = norm_g[:, None, :], b_merge[:, None, :], final_g[None, None, :]
    for layer in range(DEPTH):
        a1, a2, a3, qkv_b = _inproj(x2d, norm_g, w_in, layer, batch)
        outs = _dilated(a1, a2, a3, batch)
        ab = _moba(qkv_b, batch)
        o1 = outs[0].reshape(batch * seq, DIL_WIDTH)
        l1 = outs[3].reshape(batch * seq, DIL_WIDTH)
        x2d = _post(x2d, [o1, outs[1], outs[2]], [l1, outs[4], outs[5]], ab, norm_g, w_in, b_merge,
                    w_pa, w_pb, w_out, final_g, layer)
    return x2d.reshape(batch, seq, d)
```

```python
import functools

import numpy as np
import jax
import jax.numpy as jnp
from jax import lax
from jax.experimental import pallas as pl
from jax.experimental.pallas import tpu as pltpu

D_MODEL = 1024
SEQ = 2048
DEPTH = 2
HEAD_DIM = 64
DIL_CONFIGS = ((128, 1), (512, 4), (2048, 16))
DILATIONS = tuple(d for _, d in DIL_CONFIGS)
N_DIL_GROUPS = 3
DIL_HEADS = 4
DIL_WIDTH = DIL_HEADS * HEAD_DIM
DIL_BLK = 128
DIL_UNITS = SEQ // DIL_BLK
MOBA_HEADS = 8
MOBA_WIDTH = MOBA_HEADS * HEAD_DIM
MOBA_BLOCK = 256
MOBA_NB = SEQ // MOBA_BLOCK
MOBA_TOPK = 3
MOBA_PAIRS_PER_STEP = 2
RMS_EPS = 1e-6
NEG_INF = -1e30
QK_SCALE = HEAD_DIM ** -0.5

LANES = 128
A_QKV = 3 * DIL_WIDTH
B_QKV = 3 * MOBA_WIDTH
COL_ZA = N_DIL_GROUPS * A_QKV
COL_BQKV = COL_ZA + DIL_WIDTH
COL_ZB = COL_BQKV + B_QKV
COL_RA = COL_ZB + MOBA_WIDTH
COL_RB = COL_RA + D_MODEL
IN_COLS = COL_RB + D_MODEL
ROW_TILE = 512
INPROJ_SUBTILES = 2
VMEM_LIMIT = 56 * 1024 * 1024

F32 = jnp.float32
BF16 = jnp.bfloat16


def _slopes(n):
    return [float(np.float32(2.0 ** (-8.0 * (i + 1) / n))) for i in range(n)]


DIL_SLOPES = _slopes(N_DIL_GROUPS * DIL_HEADS)
MOBA_SLOPES = _slopes(MOBA_HEADS)
assert all(np.frexp(s)[0] == 0.5 for s in MOBA_SLOPES)


def _nt(a, b):
    return lax.dot_general(a, b, (((1,), (1,)), ((), ())), preferred_element_type=F32)


def _nn(a, b):
    return jnp.dot(a, b, preferred_element_type=F32)


def _bnt(a, b):
    return lax.dot_general(a, b, (((2,), (2,)), ((0,), (0,))), preferred_element_type=F32)


def _bnn(a, b):
    return lax.dot_general(a, b, (((2,), (1,)), ((0,), (0,))), preferred_element_type=F32)


def _rms(x, g):
    ms = jnp.mean(x * x, axis=-1, keepdims=True)
    return x * lax.rsqrt(ms + RMS_EPS) * g


def _head_masks():
    lane = lax.broadcasted_iota(jnp.int32, (1, LANES), 1)
    return lane < HEAD_DIM, lane >= HEAD_DIM


def _head_masks_bf16():
    m0, m1 = _head_masks()
    return jnp.where(m0, 1.0, 0.0).astype(BF16), jnp.where(m1, 1.0, 0.0).astype(BF16)


def _inproj_kernel(x_ref, g_ref, w_ref, a1_ref, a2_ref, a3_ref, b_ref, h_ref):
    nchunk = D_MODEL // LANES
    for t in range(INPROJ_SUBTILES):
        rows = slice(t * ROW_TILE, (t + 1) * ROW_TILE)
        h = _rms(x_ref[rows, :], g_ref[0])
        hb = h.astype(BF16)
        a1_ref[rows, :] = _nn(hb, w_ref[0, :, 0:A_QKV]).astype(BF16)
        for c0 in range(0, B_QKV, A_QKV):
            b_ref[rows, c0:c0 + A_QKV] = _nn(
                hb, w_ref[0, :, COL_BQKV + c0:COL_BQKV + c0 + A_QKV]).astype(BF16)
        for j in range(nchunk):
            h_ref[t, j] = h[:, j * LANES:(j + 1) * LANES]
        for gi, o_ref in ((1, a2_ref), (2, a3_ref)):
            d = DILATIONS[gi]
            n = ROW_TILE // d
            hp = jnp.concatenate(
                [jnp.concatenate([h_ref[t, j, pl.ds(c, n, stride=d), :] for j in range(nchunk)], axis=1)
                 for c in range(d)], axis=0).astype(BF16)
            r = _nn(hp, w_ref[0, :, gi * A_QKV:(gi + 1) * A_QKV]).astype(BF16)
            for c in range(d):
                o_ref[c, t * n:(t + 1) * n, :] = r[c * n:(c + 1) * n]


def _layer_spec(a, layer):
    return pl.BlockSpec((1,) + a.shape[1:], lambda i: (layer, 0, 0), pipeline_mode=pl.Buffered(1))


def _inproj(x2d, g, w_in, layer, batch):
    m = x2d.shape[0]
    tm = INPROJ_SUBTILES * ROW_TILE
    per_b = SEQ // tm
    out_shape = [jax.ShapeDtypeStruct((m, A_QKV), BF16)]
    out_specs = [pl.BlockSpec((tm, A_QKV), lambda i: (i, 0))]
    for d in DILATIONS[1:]:
        out_shape.append(jax.ShapeDtypeStruct((batch * d, SEQ // d, A_QKV), BF16))
        out_specs.append(pl.BlockSpec((d, tm // d, A_QKV), lambda i: (i // per_b, i % per_b, 0)))
    out_shape.append(jax.ShapeDtypeStruct((m, B_QKV), BF16))
    out_specs.append(pl.BlockSpec((tm, B_QKV), lambda i: (i, 0)))
    return pl.pallas_call(
        _inproj_kernel,
        grid=(m // tm,),
        in_specs=[pl.BlockSpec((tm, D_MODEL), lambda i: (i, 0)), _layer_spec(g, layer), _layer_spec(w_in, layer)],
        out_specs=out_specs,
        out_shape=out_shape,
        scratch_shapes=[pltpu.VMEM((INPROJ_SUBTILES, D_MODEL // LANES, ROW_TILE, LANES), F32)],
        compiler_params=pltpu.CompilerParams(dimension_semantics=("arbitrary",), vmem_limit_bytes=VMEM_LIMIT),
        name="inproj",
    )(x2d, g, w_in)


def _band_attend(q, kk, vv, bias_ref, g, pair, with_prev):
    m0, _ = _head_masks()
    hms = _head_masks_bf16()
    scale = jnp.asarray(QK_SCALE, BF16)
    pvs, ms = [], []
    vv1 = jnp.concatenate([vv, jnp.ones_like(vv)], axis=2)
    for h in range(2):
        idx = g * DIL_HEADS + pair * 2 + h
        bias = bias_ref[idx] if with_prev else bias_ref[idx, :, DIL_BLK:]
        s = _bnt(q * (hms[h] * scale), kk) + bias
        m = jnp.max(s, axis=-1, keepdims=True)
        pvs.append(_bnn(jnp.exp(s - m).astype(BF16), vv1))
        ms.append(m)
    num = jnp.where(m0, pvs[0][:, :, :LANES], pvs[1][:, :, :LANES])
    den = jnp.where(m0, pvs[0][:, :, LANES:], pvs[1][:, :, LANES:])
    return num / den, jnp.where(m0, ms[0], ms[1]) + jnp.log(den)


def _dilated_kernel(q1_ref, k1_ref, v1_ref, q2_ref, k2_ref, v2_ref, q3_ref, k3_ref, v3_ref,
                    o1_ref, o2_ref, o3_ref, l1_ref, l2_ref, l3_ref, bias_ref):
    pair = pl.program_id(1)

    @pl.when((pl.program_id(0) == 0) & (pair == 0))
    def _():
        qi = lax.broadcasted_iota(jnp.int32, (DIL_BLK, 2 * DIL_BLK), 0)
        kj = lax.broadcasted_iota(jnp.int32, (DIL_BLK, 2 * DIL_BLK), 1)
        rel = DIL_BLK + qi - kj
        valid = (rel >= 0) & (rel <= DIL_BLK)
        for g, d in enumerate(DILATIONS):
            for h in range(DIL_HEADS):
                idx = g * DIL_HEADS + h
                bias_ref[idx] = jnp.where(valid, -DIL_SLOPES[idx] * (rel * d).astype(F32), NEG_INF)

    blk = DIL_BLK
    groups = (((q1_ref, k1_ref, v1_ref), o1_ref, l1_ref, 1), ((q2_ref, k2_ref, v2_ref), o2_ref, l2_ref, 4),
              ((q3_ref, k3_ref, v3_ref), o3_ref, l3_ref, 16))
    for g, (qkv_refs, o_ref, l_ref, nres) in enumerate(groups):
        nb = DIL_UNITS // nres
        q, k, v = (r[...].reshape(nres, nb, blk, LANES) for r in qkv_refs)
        o_first, l_first = _band_attend(q[:, 0], k[:, 0], v[:, 0], bias_ref, g, pair, False)
        o_parts, l_parts = [o_first[:, None]], [l_first[:, None]]
        if nb > 1:
            e = nres * (nb - 1)
            kk = jnp.concatenate([k[:, :-1], k[:, 1:]], axis=2).reshape(e, 2 * blk, LANES)
            vv = jnp.concatenate([v[:, :-1], v[:, 1:]], axis=2).reshape(e, 2 * blk, LANES)
            o_rest, l_rest = _band_attend(q[:, 1:].reshape(e, blk, LANES), kk, vv, bias_ref, g, pair, True)
            o_parts.append(o_rest.reshape(nres, nb - 1, blk, LANES))
            l_parts.append(l_rest.reshape(nres, nb - 1, blk, LANES))
        o = jnp.concatenate(o_parts, axis=1) if nb > 1 else o_parts[0]
        l = jnp.concatenate(l_parts, axis=1) if nb > 1 else l_parts[0]
        o_ref[...] = o.reshape(nres, nb * blk, LANES).astype(BF16)
        l_ref[...] = l.reshape(nres, nb * blk, LANES)


def _dilated(a1, a2, a3, batch):
    ins, in_specs = [], []
    npairs = DIL_HEADS // 2
    for a, d in zip((a1.reshape(batch, SEQ, A_QKV), a2, a3), DILATIONS):
        for part in range(3):
            ins.append(a)
            in_specs.append(pl.BlockSpec((d, SEQ // d, LANES), lambda b, p, part=part: (b, 0, part * npairs + p)))
    out_specs = [pl.BlockSpec((d, SEQ // d, LANES), lambda b, p: (b, 0, p)) for d in DILATIONS] * 2
    out_shape = ([jax.ShapeDtypeStruct((batch * d, SEQ // d, DIL_WIDTH), BF16) for d in DILATIONS]
                 + [jax.ShapeDtypeStruct((batch * d, SEQ // d, DIL_WIDTH), F32) for d in DILATIONS])
    return pl.pallas_call(
        _dilated_kernel,
        grid=(batch, npairs),
        in_specs=in_specs,
        out_specs=out_specs,
        out_shape=out_shape,
        scratch_shapes=[pltpu.VMEM((N_DIL_GROUPS * DIL_HEADS, DIL_BLK, 2 * DIL_BLK), F32)],
        compiler_params=pltpu.CompilerParams(dimension_semantics=("arbitrary", "arbitrary"),
                                             vmem_limit_bytes=VMEM_LIMIT),
        name="dilated",
    )(*ins)


def _moba_kernel(q_ref, k_ref, v_ref, o_ref, kaug_all, vaug_all, qaug_all, slope_ref, causal_ref):
    b = pl.program_id(0)
    p = pl.program_id(1)
    blk = MOBA_BLOCK

    @pl.when((b == 0) & (p == 0))
    def _():
        for h in range(MOBA_HEADS):
            slope_ref[h:h + 1, :] = jnp.full((1, SEQ), MOBA_SLOPES[h], F32)
        qi = lax.broadcasted_iota(jnp.int32, (blk, blk), 0)
        kj = lax.broadcasted_iota(jnp.int32, (blk, blk), 1)
        causal_ref[...] = jnp.where(qi >= kj, 0.0, NEG_INF)
        row_id = lax.broadcasted_iota(jnp.int32, (SEQ, LANES), 0)
        lane_id = lax.broadcasted_iota(jnp.int32, (SEQ, LANES), 1)
        onehot = jnp.where(row_id // blk == lane_id, 1.0, 0.0)
        offset = (row_id % blk).astype(F32)
        for sp in range(MOBA_PAIRS_PER_STEP):
            kaug_all[sp, :, LANES:] = jnp.where(lane_id == MOBA_NB, offset, onehot).astype(BF16)
            vaug_all[sp, :, LANES:] = jnp.ones((SEQ, LANES), BF16)

    m0, m1 = _head_masks()
    hms = _head_masks_bf16()
    scale = jnp.asarray(QK_SCALE, BF16)
    for sp in range(MOBA_PAIRS_PER_STEP):
        _moba_pair(sp, MOBA_PAIRS_PER_STEP * p + sp, q_ref, k_ref, v_ref, o_ref, kaug_all.at[sp],
                   vaug_all.at[sp], qaug_all.at[sp], slope_ref, causal_ref, m0, m1, hms, scale)


def _moba_pair(sp, pair, q_ref, k_ref, v_ref, o_ref, kaug_ref, vaug_ref, qaug_ref, slope_ref, causal_ref,
               m0, m1, hms, scale):
    blk = MOBA_BLOCK
    sl = slice(sp * LANES, (sp + 1) * LANES)
    k_all = k_ref[0, :, sl]
    kaug_ref[:, :LANES] = k_all
    vaug_ref[:, :LANES] = v_ref[0, :, sl]

    km = jnp.concatenate([jnp.mean(k_all[j * blk:(j + 1) * blk].astype(F32), axis=0, keepdims=True)
                          for j in range(MOBA_NB)], axis=0)

    q_all = q_ref[0, :, sl]
    ridx = lax.broadcasted_iota(jnp.int32, (MOBA_NB, SEQ), 0)
    qblk = lax.broadcasted_iota(jnp.int32, (MOBA_NB, SEQ), 1) // blk
    past = ridx < qblk
    km_terms = []
    for hm in (m0, m1):
        km_h = jnp.where(hm, km, 0.0)
        km_hi = km_h.astype(BF16).astype(F32)
        km_terms += [km_hi, km_h - km_hi]
    gates = _nt(jnp.concatenate(km_terms, axis=0).astype(BF16), q_all)
    for h in range(2):
        g0 = 2 * h * MOBA_NB
        gate = jnp.where(past, gates[g0:g0 + MOBA_NB] + gates[g0 + MOBA_NB:g0 + 2 * MOBA_NB], NEG_INF)
        cnt = jnp.zeros((MOBA_NB, SEQ), jnp.int32)
        for jp in range(MOBA_NB):
            beats = (gate[jp:jp + 1, :] > gate) | ((gate[jp:jp + 1, :] == gate) & (jp < ridx))
            cnt = cnt + beats.astype(jnp.int32)
        sel = ((cnt < MOBA_TOPK) & past) | (ridx == qblk)
        slope = slope_ref[pl.ds(2 * pair + h, 1), :]
        selb = jnp.where(sel, slope * ((ridx - qblk) * blk).astype(F32), NEG_INF)
        slope_row = jnp.where(ridx == 0, slope, 0.0)
        placed = jnp.concatenate([selb, slope_row, jnp.zeros((LANES - 2 * MOBA_NB, SEQ), F32)], axis=0)
        q_h = q_all * (hms[h] * scale)
        for i in range(MOBA_NB):
            rows = slice(i * blk, (i + 1) * blk)
            qaug_ref[i, h * blk:(h + 1) * blk, :LANES] = q_h[rows]
            qaug_ref[i, h * blk:(h + 1) * blk, LANES:] = placed[:, rows].T.astype(BF16)

    def scores(i):
        nk = (i + 1) * blk
        s = _nt(qaug_ref[i], kaug_ref[0:nk, :])
        parts = []
        for h in range(2):
            sh = s[h * blk:(h + 1) * blk]
            s_own = sh[:, i * blk:] + causal_ref[...]
            m = jnp.max(s_own, axis=-1, keepdims=True)
            s_past = None
            if i > 0:
                s_past = sh[:, :i * blk]
                m = jnp.maximum(m, jnp.max(s_past, axis=-1, keepdims=True))
            parts.append((s_own, s_past, m))
        return parts

    def attend(i, parts):
        rows = slice(i * blk, (i + 1) * blk)
        pvs = []
        for s_own, s_past, m in parts:
            pv = _nn(jnp.exp(s_own - m).astype(BF16), vaug_ref[rows, :])
            if i > 0:
                pv = pv + _nn(jnp.exp(s_past - m).astype(BF16), vaug_ref[0:i * blk, :])
            pvs.append(pv)
        num = jnp.where(m0, pvs[0][:, :LANES], pvs[1][:, :LANES])
        den = jnp.where(m0, pvs[0][:, LANES:], pvs[1][:, LANES:])
        o_ref[0, rows, sl] = (num / den).astype(BF16)

    parts = scores(0)
    for i in range(1, MOBA_NB):
        nxt = scores(i)
        attend(i - 1, parts)
        parts = nxt
    attend(MOBA_NB - 1, parts)


def _moba(qkv_b, batch):
    v3 = qkv_b.reshape(batch, SEQ, B_QKV)
    pps = MOBA_PAIRS_PER_STEP
    nsteps = MOBA_HEADS // 2 // pps
    width = pps * LANES
    out = pl.pallas_call(
        _moba_kernel,
        grid=(batch, nsteps),
        in_specs=[pl.BlockSpec((1, SEQ, width), lambda b, p: (b, 0, p)),
                  pl.BlockSpec((1, SEQ, width), lambda b, p: (b, 0, nsteps + p)),
                  pl.BlockSpec((1, SEQ, width), lambda b, p: (b, 0, 2 * nsteps + p))],
        out_specs=pl.BlockSpec((1, SEQ, width), lambda b, p: (b, 0, p)),
        out_shape=jax.ShapeDtypeStruct((batch, SEQ, MOBA_WIDTH), BF16),
        scratch_shapes=[pltpu.VMEM((pps, SEQ, 2 * LANES), BF16),
                        pltpu.VMEM((pps, SEQ, 2 * LANES), BF16),
                        pltpu.VMEM((pps, MOBA_NB, 2 * MOBA_BLOCK, 2 * LANES), BF16),
                        pltpu.VMEM((MOBA_HEADS, SEQ), F32),
                        pltpu.VMEM((MOBA_BLOCK, MOBA_BLOCK), F32)],
        compiler_params=pltpu.CompilerParams(dimension_semantics=("arbitrary", "arbitrary"),
                                             vmem_limit_bytes=VMEM_LIMIT),
        name="moba",
    )(v3, v3, v3)
    return out.reshape(batch * SEQ, MOBA_WIDTH)


def _sigmoid(x):
    return 1.0 / (1.0 + jnp.exp(-x))


def _post_kernel(x_ref, o1_ref, o2_ref, o3_ref, l1_ref, l2_ref, l3_ref, ab_ref, g_ref, wzr_ref,
                 bm_ref, wpa_ref, wpb_ref, wout_ref, fg_ref, out_ref, o2s, o3s, l2s, l3s, *, final):
    x = x_ref[...]
    h = _rms(x, g_ref[0]).astype(BF16)

    def position_order(src, dst):
        d, n = src.shape[0], src.shape[1]
        for c in range(d):
            blk = src[c].astype(F32)
            for j in range(DIL_WIDTH // LANES):
                dst[j, pl.ds(c, n, stride=d), :] = blk[:, j * LANES:(j + 1) * LANES]
        return jnp.concatenate([dst[j] for j in range(DIL_WIDTH // LANES)], axis=1)

    cw = DIL_WIDTH
    nchunk = D_MODEL // cw

    def gate_chunk(w0, b0, c):
        r = _nn(h, wzr_ref[0, :, w0 + c * cw:w0 + (c + 1) * cw]) + bm_ref[0, :, b0 + c * cw:b0 + (c + 1) * cw]
        return _sigmoid(r)

    z_a = _nn(h, wzr_ref[0, :, COL_ZA:COL_ZA + DIL_WIDTH])
    gb = []
    for c in range(MOBA_WIDTH // cw):
        z_b = _nn(h, wzr_ref[0, :, COL_ZB + c * cw:COL_ZB + (c + 1) * cw])
        gb.append((ab_ref[:, c * cw:(c + 1) * cw].astype(F32) * (z_b * _sigmoid(z_b))).astype(BF16))
    gb = jnp.concatenate(gb, axis=1)

    o2, l2 = position_order(o2_ref, o2s), position_order(l2_ref, l2s)
    o3, l3 = position_order(o3_ref, o3s), position_order(l3_ref, l3s)
    l1 = l1_ref[...]
    lmax = jnp.maximum(jnp.maximum(l1, l2), l3)
    w1, w2, w3 = jnp.exp(l1 - lmax), jnp.exp(l2 - lmax), jnp.exp(l3 - lmax)
    attn_a = (w1 * o1_ref[...].astype(F32) + w2 * o2 + w3 * o3) / (w1 + w2 + w3)
    ga = (attn_a * (z_a * _sigmoid(z_a))).astype(BF16)

    gate_a = [gate_chunk(COL_RA, 0, c) for c in range(nchunk)]
    gate_b = [gate_chunk(COL_RB, D_MODEL, c) for c in range(nchunk)]
    merged = []
    for c in range(nchunk):
        cols = slice(c * cw, (c + 1) * cw)
        merged.append((gate_a[c] * _nn(ga, wpa_ref[0, :, cols])
                       + gate_b[c] * _nn(gb, wpb_ref[0, :, cols])).astype(BF16))
    xn = x + _nn(jnp.concatenate(merged, axis=1), wout_ref[0])
    if final:
        xn = _rms(xn, fg_ref[0])
    out_ref[...] = xn


def _post(x2d, o, l, ab, g, w_in, bm, w_pa, w_pb, w_out, fg, layer):
    m = x2d.shape[0]
    tm = ROW_TILE
    per_b = SEQ // tm
    row = lambda w: pl.BlockSpec((tm, w), lambda i: (i, 0))
    res = lambda d: pl.BlockSpec((d, tm // d, DIL_WIDTH), lambda i: (i // per_b, i % per_b, 0))
    grp = [row(DIL_WIDTH), res(4), res(16)]
    params = (g, w_in, bm, w_pa, w_pb, w_out)
    return pl.pallas_call(
        functools.partial(_post_kernel, final=(layer == DEPTH - 1)),
        grid=(m // tm,),
        in_specs=([row(D_MODEL)] + grp + grp + [row(MOBA_WIDTH)] + [_layer_spec(a, layer) for a in params]
                  + [_layer_spec(fg, 0)]),
        out_specs=row(D_MODEL),
        out_shape=jax.ShapeDtypeStruct((m, D_MODEL), F32),
        scratch_shapes=[pltpu.VMEM((DIL_WIDTH // LANES, tm, LANES), F32)] * 4,
        compiler_params=pltpu.CompilerParams(dimension_semantics=("arbitrary",), vmem_limit_bytes=VMEM_LIMIT),
        name="post",
    )(x2d, *o, *l, ab, *params, fg)


@jax.jit
def kernel(x, norm_g, w_in, b_merge, w_pa, w_pb, w_out, final_g):
    batch, seq, d = x.shape
    assert (seq, d) == (SEQ, D_MODEL)
    x2d = x.reshape(batch * seq, d)
    w_in, w_pa, w_pb, w_out = (w.astype(BF16) for w in (w_in, w_pa, w_pb, w_out))
    norm_g, b_merge, final_g = norm_g[:, None, :], b_merge[:, None, :], final_g[None, None, :]
    for layer in range(DEPTH):
        a1, a2, a3, qkv_b = _inproj(x2d, norm_g, w_in, layer, batch)
        outs = _dilated(a1, a2, a3, batch)
        ab = _moba(qkv_b, batch)
        o1 = outs[0].reshape(batch * seq, DIL_WIDTH)
        l1 = outs[3].reshape(batch * seq, DIL_WIDTH)
        x2d = _post(x2d, [o1, outs[1], outs[2]], [l1, outs[4], outs[5]], ab, norm_g, w_in, b_merge,
                    w_pa, w_pb, w_out, final_g, layer)
    return x2d.reshape(batch, seq, d)
```
